```python
import jax, jax.numpy as jnp
from jax import lax
import numpy as np

D_MODEL = 1024
BATCH = 4
SEQ = 4096
DEPTH = 4
DEC_BATCH = 128
DEC_SEQ = 4
PAST_LEN = 2048
PAGE_SIZE = 128

F32 = jnp.float32
D_MIX = D_MODEL
D_A = D_MIX // 4
POOL_WINDOWS = (2, 4, 8, 16)
POOL_GROUP_DIM = D_A // len(POOL_WINDOWS)
POOL_STATE = max(POOL_WINDOWS) - 1
D_B = D_MIX // 4
N_HEADS_B = 4
HEAD_DIM_B = D_B // N_HEADS_B
SGU_CHUNK = 128
D_C = D_MIX - D_A - D_B
HEAD_DIM = 64
N_HEADS_C = D_C // HEAD_DIM
Q_BLOCK = 128
ATTN_SCALE = HEAD_DIM ** -0.5
OFF_BU = D_A
OFF_BV = OFF_BU + D_B
OFF_Q = OFF_BV + D_B
OFF_K = OFF_Q + D_C
OFF_V = OFF_K + D_C
OFF_F = OFF_V + D_C
D_IN = OFF_F + N_HEADS_C
D_FF = ((8 * D_MODEL // 3 + 127) // 128) * 128
N_EXPERTS = 8
TOP_K = 2
N_DENSE = (DEPTH + 1) // 2
N_MOE = DEPTH // 2
EPS = 1e-6
NEG_INF = -1e30

kernel_name = 'hybrid_pool_sgu_fox_decoder_step'


def rms_norm(x, g):
    xf = x.astype(F32)
    y = xf * lax.rsqrt(jnp.mean(xf * xf, axis=-1, keepdims=True) + EPS)
    return (y * g.astype(F32)).astype(x.dtype)


def project(h, w_in, b_f):
    B, T, _ = h.shape
    z = jnp.einsum('btd,de->bte', h, w_in)
    a = z[..., :OFF_BU]
    u = jax.nn.gelu(z[..., OFF_BU:OFF_BV])
    v = jax.nn.gelu(z[..., OFF_BV:OFF_Q])
    q = z[..., OFF_Q:OFF_K].reshape(B, T, N_HEADS_C, HEAD_DIM)
    k = z[..., OFF_K:OFF_V].reshape(B, T, N_HEADS_C, HEAD_DIM)
    va = z[..., OFF_V:OFF_F].reshape(B, T, N_HEADS_C, HEAD_DIM)
    logf = jax.nn.log_sigmoid(z[..., OFF_F:D_IN].astype(F32) + b_f.astype(F32))
    return a, u, v, q, k, va, logf


def pool_mix(a_ext, n_prefix, pool_w, pool_scale):
    B, L, _ = a_ext.shape
    af = a_ext.astype(F32)
    c = jnp.pad(jnp.cumsum(af, axis=1), ((0, 0), (1, 0), (0, 0)))
    j = jnp.arange(n_prefix, L, dtype=jnp.int32)
    a_new = af[:, n_prefix:]
    outs = []
    for g, win in enumerate(POOL_WINDOWS):
        sl = slice(g * POOL_GROUP_DIM, (g + 1) * POOL_GROUP_DIM)
        lo = jnp.maximum(j + 1 - win, 0)
        cnt = (j + 1 - lo).astype(F32)
        mean = (c[:, j + 1, sl] - c[:, lo, sl]) / cnt[None, :, None]
        outs.append(jnp.einsum('btc,cd->btd', mean - a_new[..., sl], pool_w[g].astype(F32)))
    return (jnp.concatenate(outs, axis=-1) * pool_scale.astype(F32)).astype(a_ext.dtype)


def sgu_mix(u, v, sgu_w, sgu_b):
    B, T, _ = v.shape
    cl = min(T, SGU_CHUNK)
    nc = T // cl
    mask = jnp.tril(jnp.ones((cl, cl), dtype=bool))
    w = jnp.where(mask[None], sgu_w[:, :cl, :cl], 0).astype(v.dtype)
    vh = v.reshape(B, nc, cl, N_HEADS_B, HEAD_DIM_B)
    s = jnp.einsum('hts,bcshd->bcthd', w, vh) + jnp.transpose(sgu_b[:, :cl])[None, None, :, :, None]
    return u * s.reshape(B, T, D_B).astype(u.dtype)


def fox_attend(q, k, v, Fq, Fk, q_pos, k_pos):
    s = jnp.einsum('bqhd,bkhd->bhqk', q, k, preferred_element_type=F32) * ATTN_SCALE
    s = s + jnp.swapaxes(Fq, 1, 2)[:, :, :, None] - jnp.swapaxes(Fk, 1, 2)[:, :, None, :]
    causal = k_pos[None, :] <= q_pos[:, None]
    s = jnp.where(causal[None, None], s, NEG_INF)
    p = jax.nn.softmax(s, axis=-1)
    o = jnp.einsum('bhqk,bkhd->bqhd', p.astype(v.dtype), v, preferred_element_type=F32)
    return o.astype(q.dtype)


def fox_prompt(q, k, v, logf):
    B, S, H, Dh = q.shape
    nb = S // Q_BLOCK
    F = jnp.cumsum(logf, axis=1)
    k_pos = jnp.arange(S, dtype=jnp.int32)
    qb = jnp.moveaxis(q.reshape(B, nb, Q_BLOCK, H, Dh), 1, 0)
    Fb = jnp.moveaxis(F.reshape(B, nb, Q_BLOCK, H), 1, 0)

    def block(args):
        i, qi, Fi = args
        q_pos = i * Q_BLOCK + jnp.arange(Q_BLOCK, dtype=jnp.int32)
        return fox_attend(qi, k, v, Fi, F, q_pos, k_pos)

    o = lax.map(block, (jnp.arange(nb, dtype=jnp.int32), qb, Fb))
    return jnp.moveaxis(o, 0, 1).reshape(B, S, H * Dh)


def fox_sample(q, k, v, logf, k_past, v_past, logf_past):
    B, T, H, Dh = q.shape
    P = k_past.shape[1]
    k_all = jnp.concatenate([k_past.astype(k.dtype), k], axis=1)
    v_all = jnp.concatenate([v_past.astype(v.dtype), v], axis=1)
    F_all = jnp.cumsum(jnp.concatenate([logf_past.astype(F32), logf], axis=1), axis=1)
    q_pos = P + jnp.arange(T, dtype=jnp.int32)
    k_pos = jnp.arange(P + T, dtype=jnp.int32)
    o = fox_attend(q, k_all, v_all, F_all[:, P:], F_all, q_pos, k_pos)
    return o.reshape(B, T, H * Dh)


def merge(ya, yb, yc, gain, w_out):
    y = jnp.concatenate([rms_norm(ya, gain[:D_A]),
                         rms_norm(yb, gain[D_A:D_A + D_B]),
                         rms_norm(yc, gain[D_A + D_B:])], axis=-1)
    return jnp.einsum('bte,ed->btd', y, w_out)


def swiglu(h, wg, wu, wd):
    g = jnp.einsum('btd,df->btf', h, wg)
    up = jnp.einsum('btd,df->btf', h, wu)
    return jnp.einsum('btf,fd->btd', jax.nn.silu(g) * up, wd)


def moe_ffn(h, router, wg, wu, wd):
    logits = jnp.einsum('btd,de->bte', h, router).astype(F32)
    top_v, top_i = lax.top_k(logits, TOP_K)
    gates = jax.nn.softmax(top_v, axis=-1)
    combine = jnp.einsum('btk,btke->bte', gates, jax.nn.one_hot(top_i, N_EXPERTS, dtype=F32))
    out = jnp.zeros(h.shape, F32)
    for e in range(N_EXPERTS):
        out = out + combine[..., e:e + 1] * swiglu(h, wg[e], wu[e], wd[e]).astype(F32)
    return out.astype(h.dtype)


def channel_mixer(l, h, w_gate_dense, w_up_dense, w_down_dense, router, w_gate_moe, w_up_moe, w_down_moe):
    if l % 2 == 0:
        i = l // 2
        return swiglu(h, w_gate_dense[i], w_up_dense[i], w_down_dense[i])
    i = l // 2
    return moe_ffn(h, router[i], w_gate_moe[i], w_up_moe[i], w_down_moe[i])


def setup_inputs(seed: int = 0) -> dict:
    key = jax.random.key(seed)
    ks = jax.random.split(key, 26)
    n_pages = PAST_LEN // PAGE_SIZE
    n_used = DEC_BATCH * n_pages
    n_phys = (n_used * 5) // 4

    def nrm(k, shape, scale):
        return scale * jax.random.normal(k, shape, F32)

    page_table = jax.random.permutation(ks[6], n_phys)[:n_used].reshape(DEC_BATCH, n_pages).astype(jnp.int32)
    return {
        'x_prompt': nrm(ks[0], (BATCH, SEQ, D_MODEL), 1.0),
        'x_sample': nrm(ks[1], (DEC_BATCH, DEC_SEQ, D_MODEL), 1.0),
        'state_pool': nrm(ks[2], (DEPTH, DEC_BATCH, POOL_STATE, D_A), 1.0),
        'cache_k': nrm(ks[3], (DEPTH, n_phys, PAGE_SIZE, N_HEADS_C, HEAD_DIM), 1.0),
        'cache_v': nrm(ks[4], (DEPTH, n_phys, PAGE_SIZE, N_HEADS_C, HEAD_DIM), 1.0),
        'cache_logf': jax.nn.log_sigmoid(3.0 + nrm(ks[5], (DEPTH, n_phys, PAGE_SIZE, N_HEADS_C), 1.0)),
        'page_table': page_table,
        'norm_mix': 1.0 + nrm(ks[7], (DEPTH, D_MODEL), 0.05),
        'w_in': nrm(ks[8], (DEPTH, D_MODEL, D_IN), D_MODEL ** -0.5),
        'b_f': 3.0 + nrm(ks[9], (DEPTH, N_HEADS_C), 0.5),
        'pool_w': nrm(ks[10], (DEPTH, len(POOL_WINDOWS), POOL_GROUP_DIM, POOL_GROUP_DIM), POOL_GROUP_DIM ** -0.5),
        'pool_scale': 1.0 + nrm(ks[11], (DEPTH, D_A), 0.1),
        'sgu_w': nrm(ks[12], (DEPTH, N_HEADS_B, SGU_CHUNK, SGU_CHUNK), SGU_CHUNK ** -0.5),
        'sgu_b': 1.0 + nrm(ks[13], (DEPTH, N_HEADS_B, SGU_CHUNK), 0.1),
        'group_gain': 1.0 + nrm(ks[14], (DEPTH, D_MIX), 0.05),
        'w_out': nrm(ks[15], (DEPTH, D_MIX, D_MODEL), D_MIX ** -0.5),
        'norm_ffn': 1.0 + nrm(ks[16], (DEPTH, D_MODEL), 0.05),
        'w_gate_dense': nrm(ks[17], (N_DENSE, D_MODEL, D_FF), D_MODEL ** -0.5),
        'w_up_dense': nrm(ks[18], (N_DENSE, D_MODEL, D_FF), D_MODEL ** -0.5),
        'w_down_dense': nrm(ks[19], (N_DENSE, D_FF, D_MODEL), D_FF ** -0.5),
        'router': nrm(ks[20], (N_MOE, D_MODEL, N_EXPERTS), D_MODEL ** -0.5),
        'w_gate_moe': nrm(ks[21], (N_MOE, N_EXPERTS, D_MODEL, D_FF), D_MODEL ** -0.5),
        'w_up_moe': nrm(ks[22], (N_MOE, N_EXPERTS, D_MODEL, D_FF), D_MODEL ** -0.5),
        'w_down_moe': nrm(ks[23], (N_MOE, N_EXPERTS, D_FF, D_MODEL), D_FF ** -0.5),
        'norm_final': 1.0 + nrm(ks[24], (D_MODEL,), 0.05),
    }


def reference(x_prompt, x_sample, state_pool, cache_k, cache_v, cache_logf, page_table,
              norm_mix, w_in, b_f, pool_w, pool_scale, sgu_w, sgu_b, group_gain, w_out,
              norm_ffn, w_gate_dense, w_up_dense, w_down_dense, router, w_gate_moe, w_up_moe,
              w_down_moe, norm_final):
    xp = x_prompt
    xs = x_sample
    DB = x_sample.shape[0]
    pool_p, k_p, v_p, lf_p = [], [], [], []
    pool_s, sgu_s, k_s, v_s, lf_s = [], [], [], [], []
    for l in range(DEPTH):
        hp = rms_norm(xp, norm_mix[l])
        a, u, vb, q, k, va, lf = project(hp, w_in[l], b_f[l])
        ya = pool_mix(a, 0, pool_w[l], pool_scale[l])
        yb = sgu_mix(u, vb, sgu_w[l], sgu_b[l])
        yc = fox_prompt(q, k, va, lf)
        xp = xp + merge(ya, yb, yc, group_gain[l], w_out[l])
        pool_p.append(a[:, -POOL_STATE:])
        k_p.append(k)
        v_p.append(va)
        lf_p.append(lf)
        hs = rms_norm(xs, norm_mix[l])
        a, u, vb, q, k, va, lf = project(hs, w_in[l], b_f[l])
        a_ext = jnp.concatenate([state_pool[l].astype(a.dtype), a], axis=1)
        ya = pool_mix(a_ext, POOL_STATE, pool_w[l], pool_scale[l])
        yb = sgu_mix(u, vb, sgu_w[l], sgu_b[l])
        k_past = cache_k[l][page_table].reshape(DB, -1, N_HEADS_C, HEAD_DIM)
        v_past = cache_v[l][page_table].reshape(DB, -1, N_HEADS_C, HEAD_DIM)
        lf_past = cache_logf[l][page_table].reshape(DB, -1, N_HEADS_C)
        yc = fox_sample(q, k, va, lf, k_past, v_past, lf_past)
        xs = xs + merge(ya, yb, yc, group_gain[l], w_out[l])
        pool_s.append(a_ext[:, -POOL_STATE:])
        sgu_s.append(vb)
        k_s.append(k)
        v_s.append(va)
        lf_s.append(lf)
        xp = xp + channel_mixer(l, rms_norm(xp, norm_ffn[l]), w_gate_dense, w_up_dense, w_down_dense,
                                router, w_gate_moe, w_up_moe, w_down_moe)
        xs = xs + channel_mixer(l, rms_norm(xs, norm_ffn[l]), w_gate_dense, w_up_dense, w_down_dense,
                                router, w_gate_moe, w_up_moe, w_down_moe)
    y_prompt = rms_norm(xp, norm_final)
    y_sample = rms_norm(xs, norm_final)
    return (y_prompt, y_sample,
            jnp.stack(pool_p, 0), jnp.stack(k_p, 0), jnp.stack(v_p, 0), jnp.stack(lf_p, 0),
            jnp.stack(pool_s, 0), jnp.stack(sgu_s, 0), jnp.stack(k_s, 0), jnp.stack(v_s, 0),
            jnp.stack(lf_s, 0))
```

```python
import functools

import jax
import jax.numpy as jnp
from jax import lax
from jax.experimental import pallas as pl
from jax.experimental.pallas import tpu as pltpu

F32 = jnp.float32
BF16 = jnp.bfloat16
HIGHEST = lax.Precision.HIGHEST

D_MODEL = 1024
D_A = 256
D_B = 256
D_C = 512
HEAD_DIM = 64
N_HEADS_C = 8
N_HEADS_B = 4
POOL_WINDOWS = (2, 4, 8, 16)
POOL_GROUP_DIM = 64
POOL_STATE = 15
SGU_CHUNK = 128
D_FF = 2816
N_EXPERTS = 8
TOP_K = 2
EPS = 1e-6
NEG_INF = -1e30
ATTN_SCALE = HEAD_DIM ** -0.5

LANES = 128
HALO_ROWS = 16
OFF_U = D_A
OFF_V = OFF_U + D_B
OFF_Q = OFF_V + D_B
OFF_K = OFF_Q + D_C
OFF_VA = OFF_K + D_C
OFF_F = OFF_VA + D_C
D_IN_PAD = OFF_F + LANES
FF_CHUNK = D_FF // 2
V7X_VMEM_BYTES = 64 * 1024 * 1024
VMEM_LIMIT = (V7X_VMEM_BYTES * 7) // 8

NT_DIMS = (((1,), (1,)), ((), ()))


def _params(*sem):
    return pltpu.CompilerParams(dimension_semantics=sem, vmem_limit_bytes=VMEM_LIMIT)


def _rms(x, g):
    return x * lax.rsqrt(jnp.mean(x * x, axis=-1, keepdims=True) + EPS) * g


def _log_sigmoid(x):
    return jnp.minimum(x, 0.0) - jnp.log1p(jnp.exp(-jnp.abs(x)))


def _silu(x):
    return x / (1.0 + jnp.exp(-x))


def _div_pow2(x, d):
    assert d & (d - 1) == 0
    return lax.shift_right_logical(x, d.bit_length() - 1)


def _tile(n, pref):
    t = min(n, pref)
    while n % t:
        t //= 2
    return t


def _in_proj_kernel(x_ref, g_ref, w_ref, bf_ref, a_ref, u_ref, v_ref, q_ref, k_ref, va_ref,
                    lf_ref, *cum, tiles_per_seq):
    h = _rms(x_ref[...], g_ref[...]).astype(BF16)
    z = jnp.dot(h, w_ref[...], preferred_element_type=F32)
    a_ref[...] = z[:, :OFF_U]
    u_ref[...] = jax.nn.gelu(z[:, OFF_U:OFF_V])
    v_ref[...] = jax.nn.gelu(z[:, OFF_V:OFF_Q])
    q_ref[...] = z[:, OFF_Q:OFF_K]
    k_ref[...] = z[:, OFF_K:OFF_VA]
    va_ref[...] = z[:, OFF_VA:OFF_F]
    lf = _log_sigmoid(z[:, OFF_F:] + bf_ref[...])
    lf_ref[...] = lf[:, :N_HEADS_C]
    if cum:
        f_ref, ft_ref, carry_ref = cum
        tm = lf.shape[0]

        @pl.when(pl.program_id(0) % tiles_per_seq == 0)
        def _():
            carry_ref[...] = jnp.zeros_like(carry_ref)

        lane = lax.broadcasted_iota(jnp.int32, (1, LANES), 1)
        lf = jnp.where(lane < N_HEADS_C, lf, 0.0)
        r = lax.broadcasted_iota(jnp.int32, (LANES, LANES), 0)
        c = lax.broadcasted_iota(jnp.int32, (LANES, LANES), 1)
        tril = (r >= c).astype(F32)
        for sb in range(tm // LANES):
            rows = slice(sb * LANES, (sb + 1) * LANES)
            fb = jnp.dot(tril, lf[rows, :], precision=HIGHEST, preferred_element_type=F32) + carry_ref[...]
            f_ref[rows, :] = fb[:, :N_HEADS_C]
            ft_ref[:, rows] = jnp.transpose(fb)[:N_HEADS_C, :]
            carry_ref[...] = fb[LANES - 1:LANES, :]


def _in_proj(x, g, w, bf, *, seq_len=None):
    t = x.shape[0]
    tm = _tile(seq_len if seq_len else t, 512)
    row = lambda n: pl.BlockSpec((tm, n), lambda i: (i, 0))
    full = lambda s: pl.BlockSpec(s, lambda i: (0, 0))
    out_shape = [jax.ShapeDtypeStruct((t, n), F32) for n in (D_A, D_B, D_B, D_C, D_C, D_C, N_HEADS_C)]
    out_specs = [row(n) for n in (D_A, D_B, D_B, D_C, D_C, D_C, N_HEADS_C)]
    scratch = []
    if seq_len:
        out_shape += [jax.ShapeDtypeStruct((t, N_HEADS_C), F32), jax.ShapeDtypeStruct((N_HEADS_C, t), F32)]
        out_specs += [row(N_HEADS_C), pl.BlockSpec((N_HEADS_C, tm), lambda i: (0, i))]
        scratch = [pltpu.VMEM((1, LANES), F32)]
    return pl.pallas_call(
        functools.partial(_in_proj_kernel, tiles_per_seq=(seq_len // tm) if seq_len else 1),
        grid=(t // tm,),
        in_specs=[row(D_MODEL), full((1, D_MODEL)), full((D_MODEL, D_IN_PAD)), full((1, LANES))],
        out_specs=out_specs,
        out_shape=out_shape,
        scratch_shapes=scratch,
        compiler_params=_params("arbitrary"),
        name="in_proj",
    )(x, g, w, bf)


def _attn_prompt_kernel(q_ref, k_ref, v_ref, f_ref, ft_ref, o_ref, *, tq):
    g = pl.program_id(1)
    i = pl.program_id(2)
    lane = lax.broadcasted_iota(jnp.int32, (1, LANES), 1)
    first = lane < HEAD_DIM
    q = (q_ref[...] * ATTN_SCALE).astype(BF16)
    zero = jnp.zeros_like(q)
    q_heads = (jnp.where(first, q, zero), jnp.where(first, zero, q))
    f = f_ref[...]
    lane8 = lax.broadcasted_iota(jnp.int32, (1, N_HEADS_C), 1)
    fq = [jnp.sum(jnp.where(lane8 == 2 * g + hh, f, 0.0), axis=1, keepdims=True) for hh in range(2)]
    row = lax.broadcasted_iota(jnp.int32, (tq, tq), 0)
    col = lax.broadcasted_iota(jnp.int32, (tq, tq), 1)

    def block(j, carry, diagonal):
        m, l, acc = carry
        k0 = pl.multiple_of(j * tq, tq)
        k = k_ref[pl.ds(k0, tq), :].astype(BF16)
        v = v_ref[pl.ds(k0, tq), :].astype(BF16)
        zv = jnp.zeros_like(v)
        v_heads = (jnp.where(first, v, zv), jnp.where(first, zv, v))
        m_new, l_new, alpha, pv = [], [], [], None
        for hh in range(2):
            s = lax.dot_general(q_heads[hh], k, NT_DIMS, preferred_element_type=F32)
            fk = ft_ref[pl.ds(2 * g + hh, 1), pl.ds(k0, tq)]
            s = s + (fq[hh] - fk)
            if diagonal:
                s = jnp.where(col <= row, s, NEG_INF)
            mn = jnp.maximum(m[hh], jnp.max(s, axis=1, keepdims=True))
            al = jnp.exp(m[hh] - mn)
            p = jnp.exp(s - mn)
            m_new.append(mn)
            alpha.append(al)
            l_new.append(al * l[hh] + jnp.sum(p, axis=1, keepdims=True))
            d = jnp.dot(p.astype(BF16), v_heads[hh], preferred_element_type=F32)
            pv = d if pv is None else pv + d
        acc = acc * jnp.where(first, alpha[0], alpha[1]) + pv
        return tuple(m_new), tuple(l_new), acc

    init = ((jnp.full((tq, 1), NEG_INF, F32),) * 2, (jnp.zeros((tq, 1), F32),) * 2,
            jnp.zeros((tq, LANES), F32))
    carry = lax.fori_loop(0, i, lambda j, c: block(j, c, False), init)
    _, l, acc = block(i, carry, True)
    o_ref[...] = acc / jnp.where(first, l[0], l[1])


def _attn_prompt(q, k, v, f, ft, *, batch, seq_len):
    tq = _tile(seq_len, 256)
    nq = seq_len // tq
    qspec = pl.BlockSpec((tq, LANES), lambda b, g, i: (b * nq + i, g))
    kvspec = pl.BlockSpec((seq_len, LANES), lambda b, g, i: (b, g))
    return pl.pallas_call(
        functools.partial(_attn_prompt_kernel, tq=tq),
        grid=(batch, D_C // LANES, nq),
        in_specs=[qspec, kvspec, kvspec,
                  pl.BlockSpec((tq, N_HEADS_C), lambda b, g, i: (b * nq + i, 0)),
                  pl.BlockSpec((N_HEADS_C, seq_len), lambda b, g, i: (0, b))],
        out_specs=qspec,
        out_shape=jax.ShapeDtypeStruct(q.shape, F32),
        compiler_params=_params("arbitrary", "arbitrary", "arbitrary"),
        name="attn_prompt",
    )(q, k, v, f, ft)


def _page_copies(layer, seq, slot, pt_ref, caches, bufs, sems, n_pages):
    copies = []
    for c, (cache, buf) in enumerate(zip(caches, bufs)):
        for p in range(n_pages):
            phys = pt_ref[seq * n_pages + p]
            copies.append(pltpu.make_async_copy(cache.at[layer, phys], buf.at[slot, p], sems.at[c, slot]))
    return copies


def _attn_sample_seq(slot, local, q_ref, kn_ref, vn_ref, lfn_ref, o_ref, kbuf, vbuf, lfbuf, *, n_pages, n_new):
    page = kbuf.shape[2]
    n_rows = n_new * N_HEADS_C
    kbuf[slot, n_pages, 0:n_new, :] = kn_ref[local]
    vbuf[slot, n_pages, 0:n_new, :] = vn_ref[local]
    lfbuf[slot, n_pages, 0:n_new, :] = lfn_ref[local]

    q = q_ref[local] * ATTN_SCALE
    r8 = lax.broadcasted_iota(jnp.int32, (N_HEADS_C, D_C), 0)
    c8 = lax.broadcasted_iota(jnp.int32, (N_HEADS_C, D_C), 1)
    head_mask = _div_pow2(c8, HEAD_DIM) == r8
    qbd = jnp.concatenate(
        [jnp.where(head_mask, jnp.broadcast_to(q[t:t + 1, :], (N_HEADS_C, D_C)), 0.0) for t in range(n_new)],
        axis=0).astype(BF16)

    er = lax.broadcasted_iota(jnp.int32, (n_rows, N_HEADS_C), 0)
    ec = lax.broadcasted_iota(jnp.int32, (n_rows, N_HEADS_C), 1)
    esel = ((er & (N_HEADS_C - 1)) == ec).astype(F32)
    sr = lax.broadcasted_iota(jnp.int32, (page, page), 0)
    sc = lax.broadcasted_iota(jnp.int32, (page, page), 1)
    after = (sr > sc).astype(F32)
    carry = jnp.zeros((n_rows, 1), F32)
    bias = [None] * (n_pages + 1)
    for p in reversed(range(n_pages + 1)):
        lft = lax.dot_general(esel, lfbuf[slot, p], NT_DIMS, precision=HIGHEST, preferred_element_type=F32)
        bias[p] = jnp.dot(lft, after, precision=HIGHEST, preferred_element_type=F32) + carry
        carry = carry + jnp.sum(lft, axis=1, keepdims=True)

    scores = []
    for p in range(n_pages + 1):
        kp = kbuf[slot, p].astype(BF16)
        scores.append(lax.dot_general(qbd, kp, NT_DIMS, preferred_element_type=F32) + bias[p])
    key = lax.broadcasted_iota(jnp.int32, (n_rows, page), 1)
    qt = _div_pow2(lax.broadcasted_iota(jnp.int32, (n_rows, page), 0), N_HEADS_C)
    scores[n_pages] = jnp.where(key <= qt, scores[n_pages], NEG_INF)
    m = functools.reduce(jnp.maximum, [jnp.max(s, axis=1, keepdims=True) for s in scores])
    probs = [jnp.exp(s - m) for s in scores]
    denom = functools.reduce(jnp.add, [jnp.sum(p, axis=1, keepdims=True) for p in probs])
    out = None
    for p in range(n_pages + 1):
        d = jnp.dot(probs[p].astype(BF16), vbuf[slot, p].astype(BF16), preferred_element_type=F32)
        out = d if out is None else out + d
    out = out / denom
    for t in range(n_new):
        blk = out[t * N_HEADS_C:(t + 1) * N_HEADS_C, :]
        o_ref[local, t:t + 1, :] = jnp.sum(jnp.where(head_mask, blk, 0.0), axis=0, keepdims=True)


def _attn_sample_kernel(pt_ref, q_ref, kn_ref, vn_ref, lfn_ref, ck_hbm, cv_hbm, clf_hbm, o_ref,
                        kbuf, vbuf, lfbuf, sems, *, layer, n_pages, n_new):
    i = pl.program_id(0)
    n_steps = pl.num_programs(0)
    bufs = (kbuf, vbuf, lfbuf)
    copies = lambda seq, slot: _page_copies(layer, seq, slot, pt_ref, (ck_hbm, cv_hbm, clf_hbm), bufs,
                                            sems, n_pages)
    seq_args = (q_ref, kn_ref, vn_ref, lfn_ref, o_ref, kbuf, vbuf, lfbuf)

    @pl.when(i == 0)
    def _():
        for buf in bufs:
            for slot in range(2):
                buf[slot, n_pages] = jnp.zeros(buf.shape[2:], F32)
        for c in copies(0, 0):
            c.start()

    for c in copies(2 * i + 1, 1):
        c.start()
    for c in copies(2 * i, 0):
        c.wait()
    _attn_sample_seq(0, 0, *seq_args, n_pages=n_pages, n_new=n_new)

    @pl.when(i + 1 < n_steps)
    def _():
        for c in copies(2 * i + 2, 0):
            c.start()

    for c in copies(2 * i + 1, 1):
        c.wait()
    _attn_sample_seq(1, 1, *seq_args, n_pages=n_pages, n_new=n_new)


def _attn_sample(layer, q, k_new, v_new, lf_new, cache_k, cache_v, cache_lf, page_table):
    db, n_new, _ = q.shape
    n_pages = page_table.shape[1]
    page = cache_k.shape[2]
    assert db % 2 == 0
    new = lambda n: pl.BlockSpec((2, n_new, n), lambda b, pt: (b, 0, 0))
    hbm = pl.BlockSpec(memory_space=pl.ANY)
    grid_spec = pltpu.PrefetchScalarGridSpec(
        num_scalar_prefetch=1,
        grid=(db // 2,),
        in_specs=[new(D_C), new(D_C), new(D_C), new(N_HEADS_C), hbm, hbm, hbm],
        out_specs=new(D_C),
        scratch_shapes=[pltpu.VMEM((2, n_pages + 1, page, D_C), F32),
                        pltpu.VMEM((2, n_pages + 1, page, D_C), F32),
                        pltpu.VMEM((2, n_pages + 1, page, N_HEADS_C), F32),
                        pltpu.SemaphoreType.DMA((3, 2))],
    )
    return pl.pallas_call(
        functools.partial(_attn_sample_kernel, layer=layer, n_pages=n_pages, n_new=n_new),
        grid_spec=grid_spec,
        out_shape=jax.ShapeDtypeStruct(q.shape, F32),
        compiler_params=_params("arbitrary"),
        name="attn_sample",
    )(page_table.reshape(-1), q, k_new, v_new, lf_new, cache_k, cache_v, cache_lf)


def _pool_select(sums, cnt, a_new, wbd, scale):
    lane = _div_pow2(lax.broadcasted_iota(jnp.int32, (1, D_A), 1), POOL_GROUP_DIM)
    sel =jnp.where(lane == 0, sums[0], jnp.where(lane == 1, sums[1], jnp.where(lane == 2, sums[2], sums[3])))
    d = sel / cnt - a_new
    return jnp.dot(d.astype(BF16), wbd, preferred_element_type=F32) * scale


def _merge(ya, yb, yc, gain, w_out):
    y = jnp.concatenate([_rms(ya, gain[:, :D_A]), _rms(yb, gain[:, D_A:D_A + D_B]),
                         _rms(yc, gain[:, D_A + D_B:])], axis=-1)
    return jnp.dot(y.astype(BF16), w_out, preferred_element_type=F32)


def _window_lanes():
    lane = _div_pow2(lax.broadcasted_iota(jnp.int32, (1, D_A), 1), POOL_GROUP_DIM)
    return jnp.where(lane == 0, 2.0, jnp.where(lane == 1, 4.0, jnp.where(lane == 2, 8.0, 16.0)))


def _mix_prompt_kernel(x_ref, a_ref, halo_ref, u_ref, v_ref, yc_ref, wbd_ref, ps_ref, sw_ref, sb_ref,
                       gain_ref, wo_ref, o_ref, ext_ref, *, tiles_per_seq):
    ts = a_ref.shape[0]
    it = pl.program_id(0) % tiles_per_seq
    a = a_ref[...]
    ext_ref[0:HALO_ROWS, :] = jnp.where(it == 0, 0.0, halo_ref[...])
    ext_ref[HALO_ROWS:, :] = a
    sums, acc = [], None
    for j in range(max(POOL_WINDOWS)):
        sl = ext_ref[HALO_ROWS - j:HALO_ROWS - j + ts, :]
        acc = sl if acc is None else acc + sl
        if j + 1 in POOL_WINDOWS:
            sums.append(acc)
    pos1 = (it * ts + 1 + lax.broadcasted_iota(jnp.int32, (ts, 1), 0)).astype(F32)
    cnt = jnp.minimum(pos1, _window_lanes())
    ya = _pool_select(sums, cnt, a, wbd_ref[...], ps_ref[...])
    head = _div_pow2(lax.broadcasted_iota(jnp.int32, (1, D_B), 1), D_B // N_HEADS_B)
    ybs = []
    for c in range(ts // SGU_CHUNK):
        rows = slice(c * SGU_CHUNK, (c + 1) * SGU_CHUNK)
        vc = v_ref[rows, :].astype(BF16)
        zc = jnp.zeros_like(vc)
        s = sb_ref[...]
        for h in range(N_HEADS_B):
            s = s + jnp.dot(sw_ref[h], jnp.where(head == h, vc, zc), preferred_element_type=F32)
        ybs.append(u_ref[rows, :] * s)
    yb = jnp.concatenate(ybs, axis=0)
    o_ref[...] = x_ref[...] + _merge(ya, yb, yc_ref[...], gain_ref[...], wo_ref[...])


def _mix_prompt(x, a, u, v, yc, wbd, ps, sw, sb, gain, wo, *, seq_len):
    t = x.shape[0]
    ts = _tile(seq_len, 512)
    hpt = ts // HALO_ROWS
    row = lambda n: pl.BlockSpec((ts, n), lambda i: (i, 0))
    full = lambda s: pl.BlockSpec(s, lambda i: (0,) * len(s))
    return pl.pallas_call(
        functools.partial(_mix_prompt_kernel, tiles_per_seq=seq_len // ts),
        grid=(t // ts,),
        in_specs=[row(D_MODEL), row(D_A),
                  pl.BlockSpec((HALO_ROWS, D_A), lambda i: (jnp.maximum(i * hpt - 1, 0), 0)),
                  row(D_B), row(D_B), row(D_C), full((D_A, D_A)), full((1, D_A)),
                  full((N_HEADS_B, SGU_CHUNK, SGU_CHUNK)), full((SGU_CHUNK, D_B)),
                  full((1, D_MODEL)), full((D_MODEL, D_MODEL))],
        out_specs=row(D_MODEL),
        out_shape=jax.ShapeDtypeStruct(x.shape, F32),
        scratch_shapes=[pltpu.VMEM((ts + HALO_ROWS, D_A), F32)],
        compiler_params=_params("arbitrary"),
        name="mix_prompt",
    )(x, a, a, u, v, yc, wbd, ps, sw, sb, gain, wo)


def _mix_sample_kernel(x_ref, a_ref, st_ref, u_ref, v_ref, yc_ref, wbd_ref, ps_ref, sw_ref, sb_ref,
                       gain_ref, wo_ref, o_ref, pool_ref, *, n_new):
    def ext(p):
        if p < POOL_STATE:
            return st_ref[:, p * D_A:(p + 1) * D_A]
        return a_ref[:, (p - POOL_STATE) * D_A:(p - POOL_STATE + 1) * D_A]

    for t in range(n_new):
        j = POOL_STATE + t
        sums, acc = [], None
        for w in range(max(POOL_WINDOWS)):
            acc = ext(j - w) if acc is None else acc + ext(j - w)
            if w + 1 in POOL_WINDOWS:
                sums.append(acc)
        ya = _pool_select(sums, _window_lanes(), ext(j), wbd_ref[...], ps_ref[...])
        s = sb_ref[t:t + 1, :]
        for k in range(t + 1):
            s = s + sw_ref[t * n_new + k:t * n_new + k + 1, :] * v_ref[:, k * D_B:(k + 1) * D_B]
        yb = u_ref[:, t * D_B:(t + 1) * D_B] * s
        yc = yc_ref[:, t * D_C:(t + 1) * D_C]
        cols = slice(t * D_MODEL, (t + 1) * D_MODEL)
        o_ref[:, cols] = x_ref[:, cols] + _merge(ya, yb, yc, gain_ref[...], wo_ref[...])
    keep = POOL_STATE - n_new
    pool_ref[:, :keep * D_A] = st_ref[:, n_new * D_A:]
    pool_ref[:, keep * D_A:] = a_ref[...]


def _mix_sample(layer, x, a, state, u, v, yc, wbd, ps, sw4, sb4, gain, wo, *, n_new):
    db = x.shape[0]
    full = lambda s: pl.BlockSpec(s, lambda i: (0,) * len(s))
    return pl.pallas_call(
        functools.partial(_mix_sample_kernel, n_new=n_new),
        grid=(1,),
        in_specs=[full(x.shape), full(a.shape),
                  pl.BlockSpec((None, db, POOL_STATE * D_A), lambda i: (layer, 0, 0)),
                  full(u.shape), full(v.shape), full(yc.shape), full((D_A, D_A)), full((1, D_A)),
                  full(sw4.shape), full(sb4.shape), full((1, D_MODEL)), full((D_MODEL, D_MODEL))],
        out_specs=[full(x.shape), full((db, POOL_STATE * D_A))],
        out_shape=[jax.ShapeDtypeStruct(x.shape, F32), jax.ShapeDtypeStruct((db, POOL_STATE * D_A), F32)],
        compiler_params=_params("arbitrary"),
        name="mix_sample",
    )(x, a, state, u, v, yc, wbd, ps, sw4, sb4, gain, wo)


def _ffn_kernel(te_ref, nt_ref, x_ref, g_ref, gate_ref, wg_ref, wu_ref, wd_ref, o_ref, h_ref, acc_ref,
                *, routed):
    del te_ref
    i = pl.program_id(0)
    j = pl.program_id(1)
    live = i < nt_ref[0]

    @pl.when(jnp.logical_and(live, j == 0))
    def _():
        h_ref[...] = _rms(x_ref[...], g_ref[...]).astype(BF16)
        acc_ref[...] = jnp.zeros_like(acc_ref)

    @pl.when(live)
    def _():
        h = h_ref[...]
        gate = jnp.dot(h, wg_ref[...], preferred_element_type=F32)
        up = jnp.dot(h, wu_ref[...], preferred_element_type=F32)
        act = (_silu(gate) * up).astype(BF16)
        acc_ref[...] += jnp.dot(act, wd_ref[...], preferred_element_type=F32)

    last = j == pl.num_programs(1) - 1

    @pl.when(jnp.logical_and(live, last))
    def _():
        if routed:
            o_ref[...] = gate_ref[...] * acc_ref[...]
        else:
            o_ref[...] = x_ref[...] + acc_ref[...]

    @pl.when(jnp.logical_and(jnp.logical_not(live), last))
    def _():
        o_ref[...] = jnp.zeros_like(o_ref)


def _ffn(x, g, wg, wu, wd, li, *, tile_expert=None, n_live=None, gates=None, tm):
    routed = tile_expert is not None
    r = x.shape[0]
    n_tiles = r // tm
    if not routed:
        tile_expert = jnp.zeros((n_tiles,), jnp.int32)
        n_live = jnp.full((1,), n_tiles, jnp.int32)
        gates = jnp.ones((r, 1), F32)
        wspec_in = pl.BlockSpec((None, D_MODEL, FF_CHUNK), lambda i, j, te, nt: (li, 0, j))
        wspec_out = pl.BlockSpec((None, FF_CHUNK, D_MODEL), lambda i, j, te, nt: (li, j, 0))
    else:
        wspec_in = pl.BlockSpec((None, None, D_MODEL, FF_CHUNK), lambda i, j, te, nt: (li, te[i], 0, j))
        wspec_out = pl.BlockSpec((None, None, FF_CHUNK, D_MODEL), lambda i, j, te, nt: (li, te[i], j, 0))
    row = lambda n: pl.BlockSpec((tm, n), lambda i, j, te, nt: (i, 0))
    grid_spec = pltpu.PrefetchScalarGridSpec(
        num_scalar_prefetch=2,
        grid=(n_tiles, D_FF // FF_CHUNK),
        in_specs=[row(D_MODEL), pl.BlockSpec((1, D_MODEL), lambda i, j, te, nt: (0, 0)), row(1),
                  wspec_in, wspec_in, wspec_out],
        out_specs=row(D_MODEL),
        scratch_shapes=[pltpu.VMEM((tm, D_MODEL), BF16), pltpu.VMEM((tm, D_MODEL), F32)],
    )
    return pl.pallas_call(
        functools.partial(_ffn_kernel, routed=routed),
        grid_spec=grid_spec,
        out_shape=jax.ShapeDtypeStruct((r, D_MODEL), F32),
        compiler_params=_params("arbitrary", "arbitrary"),
        name="ffn_routed" if routed else "ffn_dense",
    )(tile_expert, n_live, x, g, gates, wg, wu, wd)


def _route_kernel(x_ref, g_ref, wr_ref, idx_ref, gate_ref):
    h = _rms(x_ref[...], g_ref[...])
    logits = jnp.dot(h, wr_ref[...], precision=HIGHEST, preferred_element_type=F32)
    lane = lax.broadcasted_iota(jnp.int32, logits.shape, 1)
    lane_f = lane.astype(F32)
    logits = jnp.where(lane < N_EXPERTS, logits, -jnp.inf)
    m1 = jnp.max(logits, axis=1, keepdims=True)
    i1 = jnp.min(jnp.where(logits == m1, lane_f, float(LANES)), axis=1, keepdims=True)
    rest = jnp.where(lane_f == i1, -jnp.inf, logits)
    m2 = jnp.max(rest, axis=1, keepdims=True)
    i2 = jnp.min(jnp.where(rest == m2, lane_f, float(LANES)), axis=1, keepdims=True)
    e2 = jnp.exp(m2 - m1)
    den = 1.0 + e2
    idx_ref[...] = jnp.where(lane == 0, i1, jnp.where(lane == 1, i2, 0.0)).astype(jnp.int32)
    gate_ref[...] = jnp.where(lane == 0, 1.0 / den, jnp.where(lane == 1, e2 / den, 0.0))


def _route(x, g, wr):
    t = x.shape[0]
    tm = _tile(t, 512)
    row = lambda n: pl.BlockSpec((tm, n), lambda i: (i, 0))
    return pl.pallas_call(
        _route_kernel,
        grid=(t // tm,),
        in_specs=[row(D_MODEL), pl.BlockSpec((1, D_MODEL), lambda i: (0, 0)),
                  pl.BlockSpec((D_MODEL, LANES), lambda i: (0, 0))],
        out_specs=[row(LANES), row(LANES)],
        out_shape=[jax.ShapeDtypeStruct((t, LANES), jnp.int32), jax.ShapeDtypeStruct((t, LANES), F32)],
        compiler_params=_params("arbitrary"),
        name="route",
    )(x, g, wr)


def _row_copy(src_hbm, src_row, dst_ref, dst_row, sem):
    return pltpu.make_async_copy(src_hbm.at[pl.ds(src_row, 1), :], dst_ref.at[pl.ds(dst_row, 1), :], sem)


def _gather_kernel(src_ref, x_hbm, o_ref, sem):
    tm = o_ref.shape[0]
    base = pl.program_id(0) * tm

    def start(r, _):
        _row_copy(x_hbm, src_ref[base + r], o_ref, r, sem).start()
        return 0

    def wait(r, _):
        _row_copy(x_hbm, 0, o_ref, r, sem).wait()
        return 0

    lax.fori_loop(0, tm, start, 0)
    lax.fori_loop(0, tm, wait, 0)


def _gather_rows(x, src, *, tm):
    r = src.shape[0]
    grid_spec = pltpu.PrefetchScalarGridSpec(
        num_scalar_prefetch=1,
        grid=(r // tm,),
        in_specs=[pl.BlockSpec(memory_space=pl.ANY)],
        out_specs=pl.BlockSpec((tm, D_MODEL), lambda i, s: (i, 0)),
        scratch_shapes=[pltpu.SemaphoreType.DMA(())],
    )
    return pl.pallas_call(
        _gather_kernel,
        grid_spec=grid_spec,
        out_shape=jax.ShapeDtypeStruct((r, D_MODEL), x.dtype),
        compiler_params=_params("arbitrary"),
        name="gather_rows",
    )(src, x)


def _combine_kernel(dst_ref, x_ref, y_hbm, g_ref, o_ref, buf_ref, sem, *, final):
    tm = x_ref.shape[0]
    base = pl.program_id(0) * tm

    def start(r, _):
        for k in range(TOP_K):
            _row_copy(y_hbm, dst_ref[(base + r) * TOP_K + k], buf_ref.at[k], r, sem).start()
        return 0

    def wait(r, _):
        for k in range(TOP_K):
            _row_copy(y_hbm, 0, buf_ref.at[k], r, sem).wait()
        return 0

    lax.fori_loop(0, tm, start, 0)
    lax.fori_loop(0, tm, wait, 0)
    out = x_ref[...] + buf_ref[0] + buf_ref[1]
    o_ref[...] = _rms(out, g_ref[...]) if final else out


def _combine(x, y, dst, g_final, *, final):
    t = x.shape[0]
    tm = _tile(t, 256)
    grid_spec = pltpu.PrefetchScalarGridSpec(
        num_scalar_prefetch=1,
        grid=(t // tm,),
        in_specs=[pl.BlockSpec((tm, D_MODEL), lambda i, d: (i, 0)), pl.BlockSpec(memory_space=pl.ANY),
                  pl.BlockSpec((1, D_MODEL), lambda i, d: (0, 0))],
        out_specs=pl.BlockSpec((tm, D_MODEL), lambda i, d: (i, 0)),
        scratch_shapes=[pltpu.VMEM((TOP_K, tm, D_MODEL), F32), pltpu.SemaphoreType.DMA(())],
    )
    return pl.pallas_call(
        functools.partial(_combine_kernel, final=final),
        grid_spec=grid_spec,
        out_shape=jax.ShapeDtypeStruct(x.shape, F32),
        compiler_params=_params("arbitrary"),
        name="combine",
    )(dst, x, y, g_final)


def _moe(x, g, wr, wg, wu, wd, li, g_final, *, tm, final):
    t = x.shape[0]
    idx, gate = _route(x, g, wr)
    e = idx[:, :TOP_K].reshape(-1)
    gv = gate[:, :TOP_K].reshape(-1)
    onehot = (e[:, None] == jnp.arange(N_EXPERTS, dtype=jnp.int32)[None, :]).astype(jnp.int32)
    rank = jnp.sum((jnp.cumsum(onehot, axis=0) - onehot) * onehot, axis=1)
    counts = jnp.sum(onehot, axis=0)
    padded = ((counts + tm - 1) // tm) * tm
    ends = jnp.cumsum(padded)
    dest = (ends - padded)[e] + rank
    n_tiles = (t * TOP_K) // tm + N_EXPERTS
    rows = n_tiles * tm
    token = jnp.arange(t * TOP_K, dtype=jnp.int32) // TOP_K
    src = jnp.zeros((rows,), jnp.int32).at[dest].set(token)
    gate_rows = jnp.zeros((rows,), F32).at[dest].set(gv)
    tile_start = jnp.arange(n_tiles, dtype=jnp.int32) * tm
    tile_expert = jnp.minimum(jnp.searchsorted(ends, tile_start, side="right"), N_EXPERTS - 1).astype(jnp.int32)
    n_live = (ends[-1:] // tm).astype(jnp.int32)
    xs = _gather_rows(x, src, tm=tm)
    ys = _ffn(xs, g, wg, wu, wd, li, tile_expert=tile_expert, n_live=n_live, gates=gate_rows[:, None], tm=tm)
    return _combine(x, ys, dest.astype(jnp.int32), g_final, final=final)


def _final_norm_kernel(x_ref, g_ref, o_ref):
    o_ref[...] = _rms(x_ref[...], g_ref[...])


def _final_norm(x, g):
    t = x.shape[0]
    tm = _tile(t, 512)
    return pl.pallas_call(
        _final_norm_kernel,
        grid=(t // tm,),
        in_specs=[pl.BlockSpec((tm, D_MODEL), lambda i: (i, 0)), pl.BlockSpec((1, D_MODEL), lambda i: (0, 0))],
        out_specs=pl.BlockSpec((tm, D_MODEL), lambda i: (i, 0)),
        out_shape=jax.ShapeDtypeStruct(x.shape, F32),
        compiler_params=_params("arbitrary"),
        name="final_norm",
    )(x, g)


def kernel(x_prompt, x_sample, state_pool, cache_k, cache_v, cache_logf, page_table, norm_mix, w_in, b_f, pool_w, pool_scale, sgu_w, sgu_b, group_gain, w_out, norm_ffn, w_gate_dense, w_up_dense, w_down_dense, router, w_gate_moe, w_up_moe, w_down_moe, norm_final):
    depth = w_in.shape[0]
    batch, seq_len, _ = x_prompt.shape
    db, n_new, _ = x_sample.shape
    n_phys, page = cache_k.shape[1], cache_k.shape[2]
    tp = batch * seq_len

    w_in_b = jnp.pad(w_in, ((0, 0), (0, 0), (0, D_IN_PAD - w_in.shape[2]))).astype(BF16)
    bf_pad = jnp.pad(b_f, ((0, 0), (0, LANES - N_HEADS_C)))[:, None, :]
    eye = jnp.eye(len(POOL_WINDOWS), dtype=F32)
    wbd = jnp.einsum("lgcd,gh->lgchd", pool_w, eye).reshape(depth, D_A, D_A).astype(BF16)
    tri = jnp.tril(jnp.ones((SGU_CHUNK, SGU_CHUNK), bool))
    sw = jnp.where(tri, sgu_w, 0.0)
    sw_b = sw.astype(BF16)
    hd_b = D_B // N_HEADS_B
    sb_exp = jnp.repeat(jnp.swapaxes(sgu_b, 1, 2), hd_b, axis=2)
    sw4 = jnp.repeat(jnp.transpose(sw[:, :, :n_new, :n_new], (0, 2, 3, 1)), hd_b, axis=3)
    sw4 = sw4.reshape(depth, n_new * n_new, D_B)
    w_out_b = w_out.astype(BF16)
    wgd, wud, wdd = (w.astype(BF16) for w in (w_gate_dense, w_up_dense, w_down_dense))
    wgm, wum, wdm = (w.astype(BF16) for w in (w_gate_moe, w_up_moe, w_down_moe))
    router_pad = jnp.pad(router, ((0, 0), (0, 0), (0, LANES - N_EXPERTS)))
    ck = cache_k.reshape(depth, n_phys, page, D_C)
    cv = cache_v.reshape(depth, n_phys, page, D_C)
    state = state_pool.reshape(depth, db, POOL_STATE * D_A)
    row = lambda v: v[None, :]

    xp = x_prompt.reshape(tp, D_MODEL)
    xs = x_sample.reshape(db, n_new * D_MODEL)
    pool_p, k_p, v_p, lf_p = [], [], [], []
    pool_s, sgu_s, k_s, v_s, lf_s = [], [], [], [], []
    for l in range(depth):
        gm, gf = row(norm_mix[l]), row(norm_ffn[l])
        a, u, vb, q, k, va, lf, f, ft = _in_proj(xp, gm, w_in_b[l], bf_pad[l], seq_len=seq_len)
        yc = _attn_prompt(q, k, va, f, ft, batch=batch, seq_len=seq_len)
        xp = _mix_prompt(xp, a, u, vb, yc, wbd[l], row(pool_scale[l]), sw_b[l], sb_exp[l],
                         row(group_gain[l]), w_out_b[l], seq_len=seq_len)
        pool_p.append(a.reshape(batch, seq_len, D_A)[:, -POOL_STATE:])
        k_p.append(k)
        v_p.append(va)
        lf_p.append(lf)
        a, u, vb, q, k, va, lf = _in_proj(xs.reshape(db * n_new, D_MODEL), gm, w_in_b[l], bf_pad[l])
        per_seq = lambda z: z.reshape(db, n_new, z.shape[-1])
        yc = _attn_sample(l, per_seq(q), per_seq(k), per_seq(va), per_seq(lf), ck, cv, cache_logf, page_table)
        lanes = lambda z: z.reshape(db, -1)
        xs, pool = _mix_sample(l, xs, lanes(a), state, lanes(u), lanes(vb), lanes(yc), wbd[l],
                               row(pool_scale[l]), sw4[l], sb_exp[l, :n_new], row(group_gain[l]),
                               w_out_b[l], n_new=n_new)
        pool_s.append(pool)
        sgu_s.append(vb)
        k_s.append(k)
        v_s.append(va)
        lf_s.append(lf)
        i = l // 2
        last = l == depth - 1
        xs = xs.reshape(db * n_new, D_MODEL)
        if l % 2 == 0:
            xp = _ffn(xp, gf, wgd, wud, wdd, i, tm=_tile(tp, 512))
            xs = _ffn(xs, gf, wgd, wud, wdd, i, tm=_tile(db * n_new, 512))
            if last:
                xp, xs = _final_norm(xp, row(norm_final)), _final_norm(xs, row(norm_final))
        else:
            gfin = row(norm_final)
            xp = _moe(xp, gf, router_pad[i], wgm, wum, wdm, i, gfin, tm=_tile(tp, 512), final=last)
            xs = _moe(xs, gf, router_pad[i], wgm, wum, wdm, i, gfin, tm=_tile(db * n_new, 128), final=last)
        xs = xs.reshape(db, n_new * D_MODEL)
    st =lambda zs, shape: jnp.stack(zs, 0).reshape((depth,) + shape)
    return (xp.reshape(batch, seq_len, D_MODEL), xs.reshape(db, n_new, D_MODEL),
            st(pool_p, (batch, POOL_STATE, D_A)),
            st(k_p, (batch, seq_len, N_HEADS_C, HEAD_DIM)), st(v_p, (batch, seq_len, N_HEADS_C, HEAD_DIM)),
            st(lf_p, (batch, seq_len, N_HEADS_C)),
            st(pool_s, (db, POOL_STATE, D_A)), st(sgu_s, (db, n_new, D_B)),
            st(k_s, (db, n_new, N_HEADS_C, HEAD_DIM)), st(v_s, (db, n_new, N_HEADS_C, HEAD_DIM)),
            st(lf_s, (db, n_new, N_HEADS_C)))
```

```python
import functools

import jax
import jax.numpy as jnp
import numpy as np
from jax import lax
from jax.experimental import pallas as pl
from jax.experimental.pallas import tpu as pltpu

F32 = jnp.float32
BF16 = jnp.bfloat16
HIGHEST = lax.Precision.HIGHEST

D_MODEL = 1024
D_A = 256
D_B = 256
D_C = 512
HEAD_DIM = 64
N_HEADS_C = 8
N_HEADS_B = 4
POOL_WINDOWS = (2, 4, 8, 16)
POOL_GROUP_DIM = 64
POOL_STATE = 15
SGU_CHUNK = 128
D_FF = 2816
N_EXPERTS = 8
TOP_K = 2
EPS = 1e-6
NEG_INF = -1e30
ATTN_SCALE = HEAD_DIM ** -0.5
LOG2E = 1.4426950408889634

LANES = 128
SUBLANES = 8
HALO_ROWS = 16
OFF_U = D_A
OFF_V = OFF_U + D_B
OFF_Q = OFF_V + D_B
OFF_K = OFF_Q + D_C
OFF_VA = OFF_K + D_C
OFF_F = OFF_VA + D_C
D_IN_PAD = OFF_F + LANES
FF_CHUNK = D_FF // 2
F_SPLIT = 3
DMA_UNROLL = 8
ATTN_TQ = 512
ATTN_TK = 512
V7X_VMEM_BYTES = 64 * 1024 * 1024
VMEM_LIMIT = (V7X_VMEM_BYTES * 7) // 8

NT_DIMS = (((1,), (1,)), ((), ()))


def _params(*sem):
    return pltpu.CompilerParams(dimension_semantics=sem, vmem_limit_bytes=VMEM_LIMIT)


def _rms(x, g):
    return x * lax.rsqrt(jnp.mean(x * x, axis=-1, keepdims=True) + EPS) * g


def _log_sigmoid(x):
    return jnp.minimum(x, 0.0) - jnp.log1p(jnp.exp(-jnp.abs(x)))


def _silu(x):
    return x / (1.0 + jnp.exp(-x))


def _div_pow2(x, d):
    assert d & (d - 1) == 0
    return lax.shift_right_logical(x, d.bit_length() - 1)


def _tile(n, pref):
    t = min(n, pref)
    while n % t:
        t //= 2
    return t


def _in_proj_kernel(x_ref, g_ref, w_ref, bf_ref, a_ref, u_ref, v_ref, q_ref, k_ref, va_ref,
                    lf_ref, *cum, tiles_per_seq):
    h = _rms(x_ref[...], g_ref[...]).astype(BF16)
    z = jnp.dot(h, w_ref[...], preferred_element_type=F32)
    a_ref[...] = z[:, :OFF_U]
    u_ref[...] = jax.nn.gelu(z[:, OFF_U:OFF_V])
    v_ref[...] = jax.nn.gelu(z[:, OFF_V:OFF_Q])
    q_ref[...] = z[:, OFF_Q:OFF_K]
    k_ref[...] = z[:, OFF_K:OFF_VA]
    va_ref[...] = z[:, OFF_VA:OFF_F]
    lf = _log_sigmoid(z[:, OFF_F:] + bf_ref[...])
    lf_ref[...] = lf[:, :N_HEADS_C]
    if cum:
        f_ref, carry_ref = cum
        tm = lf.shape[0]

        @pl.when(pl.program_id(0) % tiles_per_seq == 0)
        def _():
            carry_ref[...] = jnp.zeros_like(carry_ref)

        lane = lax.broadcasted_iota(jnp.int32, (1, LANES), 1)
        lf = jnp.where(lane < N_HEADS_C, lf, 0.0)
        r = lax.broadcasted_iota(jnp.int32, (LANES, LANES), 0)
        c = lax.broadcasted_iota(jnp.int32, (LANES, LANES), 1)
        tril = (r >= c).astype(F32)
        for sb in range(tm // LANES):
            rows = slice(sb * LANES, (sb + 1) * LANES)
            fb = jnp.dot(tril, lf[rows, :], precision=HIGHEST, preferred_element_type=F32) + carry_ref[...]
            f_ref[rows, :] = fb[:, :N_HEADS_C]
            carry_ref[...] = fb[LANES - 1:LANES, :]


def _in_proj(x, g, w, bf, *, seq_len=None):
    t = x.shape[0]
    tm = _tile(seq_len if seq_len else t, 512)
    row = lambda n: pl.BlockSpec((tm, n), lambda i: (i, 0))
    full = lambda s: pl.BlockSpec(s, lambda i: (0, 0))
    widths = (D_A, D_B, D_B, D_C, D_C, D_C, N_HEADS_C) + ((N_HEADS_C,) if seq_len else ())
    return pl.pallas_call(
        functools.partial(_in_proj_kernel, tiles_per_seq=(seq_len // tm) if seq_len else 1),
        grid=(t // tm,),
        in_specs=[row(D_MODEL), full((1, D_MODEL)), full((D_MODEL, D_IN_PAD)), full((1, LANES))],
        out_specs=[row(n) for n in widths],
        out_shape=[jax.ShapeDtypeStruct((t, n), F32) for n in widths],
        scratch_shapes=[pltpu.VMEM((1, LANES), F32)] if seq_len else [],
        compiler_params=_params("arbitrary"),
        name="in_proj",
    )(x, g, w, bf)


def _split_bf16(x):
    pieces = []
    for _ in range(F_SPLIT - 1):
        p = x.astype(BF16).astype(F32)
        pieces.append(p)
        x = x - p
    return pieces + [x]


def _attn_prompt_kernel(q_ref, k_ref, v_ref, f_ref, o_ref, kop_ref, vop_ref, acc_ref, *, tq, tk):
    g = pl.program_id(1)
    i = pl.program_id(2)
    per_q = tq // tk
    n_blocks = k_ref.shape[0] // tk
    lane = lax.broadcasted_iota(jnp.int32, (1, LANES), 1)
    lane8 = lax.broadcasted_iota(jnp.int32, (1, N_HEADS_C), 1)
    own = (lane < HEAD_DIM, lane >= HEAD_DIM)
    spare = (HEAD_DIM, 0)

    def head_col(x8, hh):
        return jnp.sum(jnp.where(lane8 == 2 * g + hh, x8, 0.0), axis=1, keepdims=True)

    def block_rows(j):
        return pl.ds(pl.multiple_of(j * tk, tk), tk)

    @pl.when(i == 0)
    def _():
        def prepare(c, _):
            rows = block_rows(c)
            kf = k_ref[rows, :]
            vf = v_ref[rows, :]
            rel = (f_ref[rows, :] - f_ref[pl.ds(pl.multiple_of(c * tk, tk), 1), :]) * LOG2E
            for hh in range(2):
                aug = 0.0
                for n, piece in enumerate(_split_bf16(head_col(rel, hh))):
                    aug = jnp.where(lane == spare[hh] + n, -piece, aug)
                kop_ref[hh, rows, :] = jnp.where(own[hh], kf, aug).astype(BF16)
                ones = jnp.where(lane == spare[hh], 1.0, 0.0)
                vop_ref[hh, rows, :] = jnp.where(own[hh], vf, ones).astype(BF16)
            return 0

        lax.fori_loop(0, n_blocks, prepare, 0)

    q = q_ref[...] * (ATTN_SCALE * LOG2E)
    q_heads = []
    for hh in range(2):
        ones = jnp.where(jnp.logical_and(lane >= spare[hh], lane < spare[hh] + F_SPLIT), 1.0, 0.0)
        q_heads.append(jnp.where(own[hh], q, ones).astype(BF16))
    fq8 = f_ref[pl.ds(pl.multiple_of(i * tq, tq), tq), :]
    fq = [head_col(fq8, hh) for hh in range(2)]
    row = lax.broadcasted_iota(jnp.int32, (tq, tk), 0)
    col = lax.broadcasted_iota(jnp.int32, (tq, tk), 1)
    acc_ref[...] = jnp.zeros_like(acc_ref)

    def scores(j):
        return tuple(lax.dot_general(q_heads[hh], kop_ref[hh, block_rows(j), :], NT_DIMS,
                                     preferred_element_type=F32) for hh in range(2))

    def consume(j, s_pair, m, first_col):
        start8 = f_ref[pl.ds(pl.multiple_of(j * tk, tk), 1), :]
        m_out, probs, alphas = [], [], []
        for hh in range(2):
            s = s_pair[hh] if first_col is None else jnp.where(col + first_col <= row, s_pair[hh], NEG_INF)
            fqj = (fq[hh] - head_col(start8, hh)) * LOG2E
            m_new = jnp.maximum(m[hh], jnp.max(s, axis=1, keepdims=True) + fqj)
            probs.append(jnp.exp2(s - (m_new - fqj)).astype(BF16))
            alphas.append(jnp.exp2(m[hh] - m_new))
            m_out.append(m_new)
        pv = [jnp.dot(probs[hh], vop_ref[hh, block_rows(j), :], preferred_element_type=F32) for hh in range(2)]
        for hh in range(2):
            acc_ref[hh] = alphas[hh] * acc_ref[hh] + pv[hh]
        return tuple(m_out)

    def step(j, carry):
        s_cur, m = carry
        s_next = scores(j + 1)
        return s_next, consume(j, s_cur, m, None)

    first = i * per_q
    m0 = (jnp.full((tq, 1), NEG_INF, F32),) * 2
    s_cur, m = lax.fori_loop(0, first, step, (scores(0), m0))
    for d in range(per_q):
        s_next = scores(first + d + 1) if d + 1 < per_q else None
        m = consume(first + d, s_cur, m, d * tk)
        s_cur = s_next
    acc0, acc1 = acc_ref[0], acc_ref[1]
    o_ref[...] = jnp.where(own[0], acc0 / acc0[:, spare[0]:spare[0] + 1], acc1 / acc1[:, spare[1]:spare[1] + 1])


def _attn_prompt(q, k, v, f, *, batch, seq_len):
    tq = _tile(seq_len, ATTN_TQ)
    tk = _tile(tq, ATTN_TK)
    nq = seq_len // tq
    qspec = pl.BlockSpec((tq, LANES), lambda b, g, i: (b * nq + i, g))
    kvspec = pl.BlockSpec((seq_len, LANES), lambda b, g, i: (b, g))
    return pl.pallas_call(
        functools.partial(_attn_prompt_kernel, tq=tq, tk=tk),
        grid=(batch, D_C // LANES, nq),
        in_specs=[qspec, kvspec, kvspec, pl.BlockSpec((seq_len, N_HEADS_C), lambda b, g, i: (b, 0))],
        out_specs=qspec,
        out_shape=jax.ShapeDtypeStruct(q.shape, F32),
        scratch_shapes=[pltpu.VMEM((2, seq_len, LANES), BF16), pltpu.VMEM((2, seq_len, LANES), BF16),
                        pltpu.VMEM((2, tq, LANES), F32)],
        compiler_params=_params("arbitrary", "arbitrary", "arbitrary"),
        name="attn_prompt",
    )(q, k, v, f)


def _forget_consts():
    c = np.arange(LANES)
    same = (c[:, None] % N_HEADS_C) == (c[None, :] % N_HEADS_C)
    later_in_row = np.logical_and(c[:, None] > c[None, :], same)
    return later_in_row.astype(np.float32), same.astype(np.float32)


def _page_copies(layer, seq, slot, pt_ref, caches, bufs, sems, n_pages, lf_rows):
    (ck, cv, clf), (kbuf, vbuf, lfbuf) = caches, bufs
    copies = []
    for p in range(n_pages):
        phys = pt_ref[seq * n_pages + p]
        copies.append(pltpu.make_async_copy(ck.at[layer, phys], kbuf.at[slot, p], sems.at[0, slot]))
        copies.append(pltpu.make_async_copy(cv.at[layer, phys], vbuf.at[slot, p], sems.at[1, slot]))
        copies.append(pltpu.make_async_copy(clf.at[layer, phys], lfbuf.at[slot, pl.ds(p * lf_rows, lf_rows)],
                                            sems.at[2, slot]))
    return copies


def _attn_sample_seq(slot, local, q_ref, kn_ref, vn_ref, lfn_ref, later_ref, same_ref, o_ref,
                     kbuf, vbuf, lfbuf, *, n_pages, n_new):
    page = kbuf.shape[2]
    lf_rows = page * N_HEADS_C // LANES
    keys_per_row = LANES // N_HEADS_C
    n_rows = n_new * N_HEADS_C
    kbuf[slot, n_pages, 0:n_new] = kn_ref[local]
    vbuf[slot, n_pages, 0:n_new] = vn_ref[local]
    lfbuf[slot, n_pages * lf_rows:n_pages * lf_rows + 1, 0:n_rows] = lfn_ref[local]

    x = lfbuf[slot]
    n_flat = x.shape[0]
    r0 = lax.broadcasted_iota(jnp.int32, (n_flat, n_flat), 0)
    r1 = lax.broadcasted_iota(jnp.int32, (n_flat, n_flat), 1)
    later_rows = (r1 > r0).astype(F32)
    head_tot = jnp.dot(x, same_ref[...], precision=HIGHEST, preferred_element_type=F32)
    g_all = (jnp.dot(x, later_ref[...], precision=HIGHEST, preferred_element_type=F32)
             + jnp.dot(later_rows, head_tot, precision=HIGHEST, preferred_element_type=F32))

    q = (q_ref[local] * ATTN_SCALE).reshape(n_rows, HEAD_DIM).astype(BF16)
    qrow = lax.broadcasted_iota(jnp.int32, (n_rows, LANES), 0)
    klane = lax.broadcasted_iota(jnp.int32, (n_rows, LANES), 1)
    own_head = (qrow & (N_HEADS_C - 1)) == (klane & (N_HEADS_C - 1))
    causal_new = jnp.logical_and(own_head, _div_pow2(klane, N_HEADS_C) <= _div_pow2(qrow, N_HEADS_C))

    def scores(kp, g_rows, mask):
        s = lax.dot_general(q, kp.astype(BF16), NT_DIMS, preferred_element_type=F32)
        tiles = [jnp.where(mask, s[:, r * LANES:(r + 1) * LANES] + g_rows[r:r + 1, :], NEG_INF)
                 for r in range(g_rows.shape[0])]
        return jnp.concatenate(tiles, axis=1) if len(tiles) > 1 else tiles[0]

    s_pages = [scores(kbuf[slot, p].reshape(page * N_HEADS_C, HEAD_DIM),
                      g_all[p * lf_rows:(p + 1) * lf_rows, :], own_head) for p in range(n_pages)]
    k_new = kbuf[slot, n_pages, 0:keys_per_row].reshape(LANES, HEAD_DIM)
    s_pages.append(scores(k_new, g_all[n_pages * lf_rows:n_pages * lf_rows + 1, :], causal_new))
    m = functools.reduce(jnp.maximum, [jnp.max(s, axis=1, keepdims=True) for s in s_pages])
    probs = [jnp.exp(s - m) for s in s_pages]
    denom = functools.reduce(jnp.add, [jnp.sum(p, axis=1, keepdims=True) for p in probs])
    out = None
    for p in range(n_pages + 1):
        if p < n_pages:
            vp = vbuf[slot, p].reshape(page * N_HEADS_C, HEAD_DIM)
        else:
            vp = vbuf[slot, n_pages, 0:keys_per_row].reshape(LANES, HEAD_DIM)
        d = jnp.dot(probs[p].astype(BF16), vp.astype(BF16), preferred_element_type=F32)
        out = d if out is None else out + d
    o_ref[local] = (out / denom).reshape(n_new, N_HEADS_C, HEAD_DIM)


def _attn_sample_kernel(pt_ref, q_ref, kn_ref, vn_ref, lfn_ref, later_ref, same_ref, ck_hbm, cv_hbm, clf_hbm,
                        o_ref, kbuf, vbuf, lfbuf, sems, *, layer, n_pages, n_new):
    i = pl.program_id(0)
    n_steps = pl.num_programs(0)
    bufs = (kbuf, vbuf, lfbuf)
    lf_rows = kbuf.shape[2] * N_HEADS_C // LANES
    copies = lambda seq, slot: _page_copies(layer, seq, slot, pt_ref, (ck_hbm, cv_hbm, clf_hbm), bufs,
                                            sems, n_pages, lf_rows)
    seq = functools.partial(_attn_sample_seq, q_ref=q_ref, kn_ref=kn_ref, vn_ref=vn_ref, lfn_ref=lfn_ref,
                            later_ref=later_ref, same_ref=same_ref, o_ref=o_ref, kbuf=kbuf, vbuf=vbuf,
                            lfbuf=lfbuf, n_pages=n_pages, n_new=n_new)

    @pl.when(i == 0)
    def _():
        for slot in range(2):
            kbuf[slot, n_pages] = jnp.zeros(kbuf.shape[2:], F32)
            vbuf[slot, n_pages] = jnp.zeros(vbuf.shape[2:], F32)
            lfbuf[slot, n_pages * lf_rows:, :] = jnp.zeros((lfbuf.shape[1] - n_pages * lf_rows, LANES), F32)
        for c in copies(0, 0):
            c.start()

    for c in copies(2 * i + 1, 1):
        c.start()
    for c in copies(2 * i, 0):
        c.wait()
    seq(0, 0)

    @pl.when(i + 1 < n_steps)
    def _():
        for c in copies(2 * i + 2, 0):
            c.start()

    for c in copies(2 * i + 1, 1):
        c.wait()
    seq(1, 1)


def _attn_sample(layer, q, k_new, v_new, lf_new, cache_k, cache_v, cache_lf, page_table):
    db, n_new = q.shape[:2]
    n_pages = page_table.shape[1]
    page = cache_k.shape[2]
    lf_rows = cache_lf.shape[2]
    assert db % 2 == 0 and n_new * N_HEADS_C <= LANES and n_new <= LANES // N_HEADS_C
    flat_rows = -(-((n_pages + 1) * lf_rows) // LANES) * LANES
    later, same = _forget_consts()
    new = pl.BlockSpec((2, n_new, N_HEADS_C, HEAD_DIM), lambda b, pt: (b, 0, 0, 0))
    const = pl.BlockSpec((LANES, LANES), lambda b, pt: (0, 0))
    hbm = pl.BlockSpec(memory_space=pl.ANY)
    grid_spec = pltpu.PrefetchScalarGridSpec(
        num_scalar_prefetch=1,
        grid=(db // 2,),
        in_specs=[new, new, new, pl.BlockSpec((2, 1, n_new * N_HEADS_C), lambda b, pt: (b, 0, 0)),
                  const, const, hbm, hbm, hbm],
        out_specs=new,
        scratch_shapes=[pltpu.VMEM((2, n_pages + 1, page, N_HEADS_C, HEAD_DIM), F32),
                        pltpu.VMEM((2, n_pages + 1, page, N_HEADS_C, HEAD_DIM), F32),
                        pltpu.VMEM((2, flat_rows, LANES), F32),
                        pltpu.SemaphoreType.DMA((3, 2))],
    )
    return pl.pallas_call(
        functools.partial(_attn_sample_kernel, layer=layer, n_pages=n_pages, n_new=n_new),
        grid_spec=grid_spec,
        out_shape=jax.ShapeDtypeStruct(q.shape, F32),
        compiler_params=_params("arbitrary"),
        name="attn_sample",
    )(page_table.reshape(-1), q, k_new, v_new, lf_new, jnp.asarray(later), jnp.asarray(same),
      cache_k, cache_v, cache_lf)


def _pool_select(sums, cnt, a_new, wbd, scale):
    lane = _div_pow2(lax.broadcasted_iota(jnp.int32, (1, D_A), 1), POOL_GROUP_DIM)
    sel = jnp.where(lane == 0, sums[0], jnp.where(lane == 1, sums[1], jnp.where(lane == 2, sums[2], sums[3])))
    d = sel / cnt - a_new
    return jnp.dot(d.astype(BF16), wbd, preferred_element_type=F32) * scale


def _merge(ya, yb, yc, gain, w_out):
    y = jnp.concatenate([_rms(ya, gain[:, :D_A]), _rms(yb, gain[:, D_A:D_A + D_B]),
                         _rms(yc, gain[:, D_A + D_B:])], axis=-1)
    return jnp.dot(y.astype(BF16), w_out, preferred_element_type=F32)


def _window_lanes():
    lane = _div_pow2(lax.broadcasted_iota(jnp.int32, (1, D_A), 1), POOL_GROUP_DIM)
    return jnp.where(lane == 0, 2.0, jnp.where(lane == 1, 4.0, jnp.where(lane == 2, 8.0, 16.0)))


def _mix_prompt_kernel(x_ref, a_ref, halo_ref, u_ref, v_ref, yc_ref, wbd_ref, ps_ref, sw_ref, sb_ref,
                       gain_ref, wo_ref, o_ref, ext_ref, *, tiles_per_seq):
    ts = a_ref.shape[0]
    it = pl.program_id(0) % tiles_per_seq
    a = a_ref[...]
    ext_ref[0:HALO_ROWS, :] = jnp.where(it == 0, 0.0, halo_ref[...])
    ext_ref[HALO_ROWS:, :] = a
    sums, acc = [], None
    for j in range(max(POOL_WINDOWS)):
        sl = ext_ref[HALO_ROWS - j:HALO_ROWS - j + ts, :]
        acc = sl if acc is None else acc + sl
        if j + 1 in POOL_WINDOWS:
            sums.append(acc)
    pos1 = (it * ts + 1 + lax.broadcasted_iota(jnp.int32, (ts, 1), 0)).astype(F32)
    cnt = jnp.minimum(pos1, _window_lanes())
    ya = _pool_select(sums, cnt, a, wbd_ref[...], ps_ref[...])
    head = _div_pow2(lax.broadcasted_iota(jnp.int32, (1, D_B), 1), D_B // N_HEADS_B)
    ybs = []
    for c in range(ts // SGU_CHUNK):
        rows = slice(c * SGU_CHUNK, (c + 1) * SGU_CHUNK)
        vc = v_ref[rows, :].astype(BF16)
        zc = jnp.zeros_like(vc)
        s = sb_ref[...]
        for h in range(N_HEADS_B):
            s = s + jnp.dot(sw_ref[h], jnp.where(head == h, vc, zc), preferred_element_type=F32)
        ybs.append(u_ref[rows, :] * s)
    yb = jnp.concatenate(ybs, axis=0)
    o_ref[...] = x_ref[...] + _merge(ya, yb, yc_ref[...], gain_ref[...], wo_ref[...])


def _mix_prompt(x, a, u, v, yc, wbd, ps, sw, sb, gain, wo, *, seq_len):
    t = x.shape[0]
    ts = _tile(seq_len, 512)
    hpt = ts // HALO_ROWS
    row = lambda n: pl.BlockSpec((ts, n), lambda i: (i, 0))
    full = lambda s: pl.BlockSpec(s, lambda i: (0,) * len(s))
    return pl.pallas_call(
        functools.partial(_mix_prompt_kernel, tiles_per_seq=seq_len // ts),
        grid=(t // ts,),
        in_specs=[row(D_MODEL), row(D_A),
                  pl.BlockSpec((HALO_ROWS, D_A), lambda i: (jnp.maximum(i * hpt - 1, 0), 0)),
                  row(D_B), row(D_B), row(D_C), full((D_A, D_A)), full((1, D_A)),
                  full((N_HEADS_B, SGU_CHUNK, SGU_CHUNK)), full((SGU_CHUNK, D_B)),
                  full((1, D_MODEL)), full((D_MODEL, D_MODEL))],
        out_specs=row(D_MODEL),
        out_shape=jax.ShapeDtypeStruct(x.shape, F32),
        scratch_shapes=[pltpu.VMEM((ts + HALO_ROWS, D_A), F32)],
        compiler_params=_params("arbitrary"),
        name="mix_prompt",
    )(x, a, a, u, v, yc, wbd, ps, sw, sb, gain, wo)


def _mix_sample_kernel(x_ref, a_ref, st_ref, u_ref, v_ref, yc_ref, wbd_ref, ps_ref, sw_ref, sb_ref,
                       gain_ref, wo_ref, o_ref, pool_ref, *, n_new):
    def ext(p):
        if p < POOL_STATE:
            return st_ref[:, p * D_A:(p + 1) * D_A]
        return a_ref[:, (p - POOL_STATE) * D_A:(p - POOL_STATE + 1) * D_A]

    for t in range(n_new):
        j = POOL_STATE + t
        sums, acc = [], None
        for w in range(max(POOL_WINDOWS)):
            acc = ext(j - w) if acc is None else acc + ext(j - w)
            if w + 1 in POOL_WINDOWS:
                sums.append(acc)
        ya = _pool_select(sums, _window_lanes(), ext(j), wbd_ref[...], ps_ref[...])
        s = sb_ref[t:t + 1, :]
        for k in range(t + 1):
            s = s + sw_ref[t * n_new + k:t * n_new + k + 1, :] * v_ref[:, k * D_B:(k + 1) * D_B]
        yb = u_ref[:, t * D_B:(t + 1) * D_B] * s
        yc = yc_ref[:, t * D_C:(t + 1) * D_C]
        cols = slice(t * D_MODEL, (t + 1) * D_MODEL)
        o_ref[:, cols] = x_ref[:, cols] + _merge(ya, yb, yc, gain_ref[...], wo_ref[...])
    keep = POOL_STATE - n_new
    pool_ref[:, :keep * D_A] = st_ref[:, n_new * D_A:]
    pool_ref[:, keep * D_A:] = a_ref[...]


def _mix_sample(layer, x, a, state, u, v, yc, wbd, ps, sw4, sb4, gain, wo, *, n_new):
    db = x.shape[0]
    full = lambda s: pl.BlockSpec(s, lambda i: (0,) * len(s))
    return pl.pallas_call(
        functools.partial(_mix_sample_kernel, n_new=n_new),
        grid=(1,),
        in_specs=[full(x.shape), full(a.shape),
                  pl.BlockSpec((None, db, POOL_STATE * D_A), lambda i: (layer, 0, 0)),
                  full(u.shape), full(v.shape), full(yc.shape), full((D_A, D_A)), full((1, D_A)),
                  full(sw4.shape), full(sb4.shape), full((1, D_MODEL)), full((D_MODEL, D_MODEL))],
        out_specs=[full(x.shape), full((db, POOL_STATE * D_A))],
        out_shape=[jax.ShapeDtypeStruct(x.shape, F32), jax.ShapeDtypeStruct((db, POOL_STATE * D_A), F32)],
        compiler_params=_params("arbitrary"),
        name="mix_sample",
    )(x, a, state, u, v, yc, wbd, ps, sw4, sb4, gain, wo)


def _ffn_kernel(te_ref, nt_ref, x_ref, g_ref, wg_ref, wu_ref, wd_ref, o_ref, h_ref, acc_ref, *, routed):
    del te_ref
    i = pl.program_id(0)
    j = pl.program_id(1)
    live = i < nt_ref[0]
    last = j == pl.num_programs(1) - 1

    @pl.when(jnp.logical_and(live, j == 0))
    def _():
        h_ref[...] = _rms(x_ref[...], g_ref[...]).astype(BF16)
        acc_ref[...] = jnp.zeros_like(acc_ref)

    @pl.when(live)
    def _():
        h = h_ref[...]
        gate = jnp.dot(h, wg_ref[...], preferred_element_type=F32)
        up = jnp.dot(h, wu_ref[...], preferred_element_type=F32)
        act = (_silu(gate) * up).astype(BF16)
        acc_ref[...] += jnp.dot(act, wd_ref[...], preferred_element_type=F32)

    @pl.when(jnp.logical_and(live, last))
    def _():
        o_ref[...] = acc_ref[...] if routed else x_ref[...] + acc_ref[...]

    @pl.when(jnp.logical_and(jnp.logical_not(live), last))
    def _():
        o_ref[...] = jnp.zeros_like(o_ref)


def _ffn(x, g, wg, wu, wd, li, *, tile_expert=None, n_live=None, tm):
    routed = tile_expert is not None
    r = x.shape[0]
    n_tiles = r // tm
    if not routed:
        tile_expert = jnp.zeros((n_tiles,), jnp.int32)
        n_live = jnp.full((1,), n_tiles, jnp.int32)
        wspec_in = pl.BlockSpec((None, D_MODEL, FF_CHUNK), lambda i, j, te, nt: (li, 0, j))
        wspec_out = pl.BlockSpec((None, FF_CHUNK, D_MODEL), lambda i, j, te, nt: (li, j, 0))
    else:
        wspec_in = pl.BlockSpec((None, None, D_MODEL, FF_CHUNK), lambda i, j, te, nt: (li, te[i], 0, j))
        wspec_out = pl.BlockSpec((None, None, FF_CHUNK, D_MODEL), lambda i, j, te, nt: (li, te[i], j, 0))
    row = pl.BlockSpec((tm, D_MODEL), lambda i, j, te, nt: (i, 0))
    grid_spec = pltpu.PrefetchScalarGridSpec(
        num_scalar_prefetch=2,
        grid=(n_tiles, D_FF // FF_CHUNK),
        in_specs=[row, pl.BlockSpec((1, D_MODEL), lambda i, j, te, nt: (0, 0)), wspec_in, wspec_in, wspec_out],
        out_specs=row,
        scratch_shapes=[pltpu.VMEM((tm, D_MODEL), BF16), pltpu.VMEM((tm, D_MODEL), F32)],
    )
    return pl.pallas_call(
        functools.partial(_ffn_kernel, routed=routed),
        grid_spec=grid_spec,
        out_shape=jax.ShapeDtypeStruct((r, D_MODEL), F32),
        compiler_params=_params("arbitrary", "arbitrary"),
        name="ffn_routed" if routed else "ffn_dense",
    )(tile_expert, n_live, x, g, wg, wu, wd)


def _route_kernel(x_ref, g_ref, wr_ref, idx_ref, gate_ref):
    h = _rms(x_ref[...], g_ref[...])
    logits = jnp.dot(h, wr_ref[...], precision=HIGHEST, preferred_element_type=F32)
    lane = lax.broadcasted_iota(jnp.int32, logits.shape, 1)
    lane_f = lane.astype(F32)
    logits = jnp.where(lane < N_EXPERTS, logits, -jnp.inf)
    m1 = jnp.max(logits, axis=1, keepdims=True)
    i1 = jnp.min(jnp.where(logits == m1, lane_f, float(LANES)), axis=1, keepdims=True)
    rest = jnp.where(lane_f == i1, -jnp.inf, logits)
    m2 = jnp.max(rest, axis=1, keepdims=True)
    i2 = jnp.min(jnp.where(rest == m2, lane_f, float(LANES)), axis=1, keepdims=True)
    e2 = jnp.exp(m2 - m1)
    den = 1.0 + e2
    idx_ref[...] = jnp.where(lane == 0, i1, jnp.where(lane == 1, i2, 0.0)).astype(jnp.int32)
    gate_ref[...] = jnp.where(lane == 0, 1.0 / den, jnp.where(lane == 1, e2 / den, 0.0))


def _route(x, g, wr):
    t = x.shape[0]
    tm = _tile(t, 512)
    row = lambda n: pl.BlockSpec((tm, n), lambda i: (i, 0))
    return pl.pallas_call(
        _route_kernel,
        grid=(t // tm,),
        in_specs=[row(D_MODEL), pl.BlockSpec((1, D_MODEL), lambda i: (0, 0)),
                  pl.BlockSpec((D_MODEL, LANES), lambda i: (0, 0))],
        out_specs=[row(LANES), row(LANES)],
        out_shape=[jax.ShapeDtypeStruct((t, LANES), jnp.int32), jax.ShapeDtypeStruct((t, LANES), F32)],
        compiler_params=_params("arbitrary"),
        name="route",
    )(x, g, wr)


def _dispatch_kernel(dst_ref, x_ref, init_hbm, o_hbm, sem):
    del init_hbm
    tm = x_ref.shape[0]
    base = pl.program_id(0) * tm

    def copy(r, k, slot):
        return pltpu.make_async_copy(x_ref.at[pl.ds(r, 1), :], o_hbm.at[pl.ds(slot, 1), :], sem)

    def start(r, _):
        for k in range(TOP_K):
            copy(r, k, dst_ref[(base + r) * TOP_K + k]).start()
        return 0

    def wait(r, _):
        for k in range(TOP_K):
            copy(r, k, 0).wait()
        return 0

    lax.fori_loop(0, tm, start, 0, unroll=DMA_UNROLL)
    lax.fori_loop(0, tm, wait, 0, unroll=DMA_UNROLL)


def _dispatch(x, dst, rows):
    t = x.shape[0]
    tm = _tile(t, 256)
    hbm = pl.BlockSpec(memory_space=pl.ANY)
    grid_spec = pltpu.PrefetchScalarGridSpec(
        num_scalar_prefetch=1,
        grid=(t // tm,),
        in_specs=[pl.BlockSpec((tm, D_MODEL), lambda i, d: (i, 0)), hbm],
        out_specs=hbm,
        scratch_shapes=[pltpu.SemaphoreType.DMA(())],
    )
    return pl.pallas_call(
        _dispatch_kernel,
        grid_spec=grid_spec,
        out_shape=jax.ShapeDtypeStruct((rows, D_MODEL), F32),
        input_output_aliases={2: 0},
        compiler_params=_params("arbitrary"),
        name="dispatch",
    )(dst, x, jnp.zeros((rows, D_MODEL), F32))


def _combine_kernel(dst_ref, x_ref, gate_ref, y_hbm, g_ref, o_ref, buf_ref, sem, *, final):
    tm = x_ref.shape[0]
    base = pl.program_id(0) * tm

    def copy(r, k, slot):
        return pltpu.make_async_copy(y_hbm.at[pl.ds(slot, 1), :], buf_ref.at[k, pl.ds(r, 1), :], sem)

    def start(r, _):
        for k in range(TOP_K):
            copy(r, k, dst_ref[(base + r) * TOP_K + k]).start()
        return 0

    def wait(r, _):
        for k in range(TOP_K):
            copy(r, k, 0).wait()
        return 0

    lax.fori_loop(0, tm, start, 0, unroll=DMA_UNROLL)
    lax.fori_loop(0, tm, wait, 0, unroll=DMA_UNROLL)
    gate = gate_ref[...]
    out = x_ref[...]
    for k in range(TOP_K):
        out = out + gate[:, k:k + 1] * buf_ref[k]
    o_ref[...] = _rms(out, g_ref[...]) if final else out


def _combine(x, gate, y, dst, g_final, *, final):
    t = x.shape[0]
    tm = _tile(t, 256)
    row = lambda n: pl.BlockSpec((tm, n), lambda i, d: (i, 0))
    grid_spec = pltpu.PrefetchScalarGridSpec(
        num_scalar_prefetch=1,
        grid=(t // tm,),
        in_specs=[row(D_MODEL), row(LANES), pl.BlockSpec(memory_space=pl.ANY),
                  pl.BlockSpec((1, D_MODEL), lambda i, d: (0, 0))],
        out_specs=row(D_MODEL),
        scratch_shapes=[pltpu.VMEM((TOP_K, tm, D_MODEL), F32), pltpu.SemaphoreType.DMA(())],
    )
    return pl.pallas_call(
        functools.partial(_combine_kernel, final=final),
        grid_spec=grid_spec,
        out_shape=jax.ShapeDtypeStruct(x.shape, F32),
        compiler_params=_params("arbitrary"),
        name="combine",
    )(dst, x, gate, y, g_final)


def _moe(x, g, wr, wg, wu, wd, li, g_final, *, tm, final):
    t = x.shape[0]
    idx, gate = _route(x, g, wr)
    e = idx[:, :TOP_K].reshape(-1)
    onehot = (e[:, None] == jnp.arange(N_EXPERTS, dtype=jnp.int32)[None, :]).astype(jnp.int32)
    rank = jnp.sum((jnp.cumsum(onehot, axis=0) - onehot) * onehot, axis=1)
    counts = jnp.sum(onehot, axis=0)
    padded = ((counts + tm - 1) // tm) * tm
    ends = jnp.cumsum(padded)
    dest = (jnp.sum(onehot * (ends - padded)[None, :], axis=1) + rank).astype(jnp.int32)
    n_tiles = (t * TOP_K) // tm + N_EXPERTS
    tile_start = jnp.arange(n_tiles, dtype=jnp.int32) * tm
    tile_expert = jnp.sum((ends[None, :] <= tile_start[:, None]).astype(jnp.int32), axis=1)
    tile_expert = jnp.minimum(tile_expert, N_EXPERTS - 1)
    n_live = (ends[-1:] // tm).astype(jnp.int32)
    xs = _dispatch(x, dest, n_tiles * tm)
    ys = _ffn(xs, g, wg, wu, wd, li, tile_expert=tile_expert, n_live=n_live, tm=tm)
    return _combine(x, gate, ys, dest, g_final, final=final)


def _final_norm_kernel(x_ref, g_ref, o_ref):
    o_ref[...] = _rms(x_ref[...], g_ref[...])


def _final_norm(x, g):
    t = x.shape[0]
    tm = _tile(t, 512)
    return pl.pallas_call(
        _final_norm_kernel,
        grid=(t // tm,),
        in_specs=[pl.BlockSpec((tm, D_MODEL), lambda i: (i, 0)), pl.BlockSpec((1, D_MODEL), lambda i: (0, 0))],
        out_specs=pl.BlockSpec((tm, D_MODEL), lambda i: (i, 0)),
        out_shape=jax.ShapeDtypeStruct(x.shape, F32),
        compiler_params=_params("arbitrary"),
        name="final_norm",
    )(x, g)


def kernel(x_prompt, x_sample, state_pool, cache_k, cache_v, cache_logf, page_table, norm_mix, w_in, b_f, pool_w, pool_scale, sgu_w, sgu_b, group_gain, w_out, norm_ffn, w_gate_dense, w_up_dense, w_down_dense, router, w_gate_moe, w_up_moe, w_down_moe, norm_final):
    depth = w_in.shape[0]
    batch, seq_len, _ = x_prompt.shape
    db, n_new, _ = x_sample.shape
    n_phys, page = cache_k.shape[1], cache_k.shape[2]
    tp = batch * seq_len

    w_in_b = jnp.pad(w_in, ((0, 0), (0, 0), (0, D_IN_PAD - w_in.shape[2]))).astype(BF16)
    bf_pad = jnp.pad(b_f, ((0, 0), (0, LANES - N_HEADS_C)))[:, None, :]
    eye = jnp.eye(len(POOL_WINDOWS), dtype=F32)
    wbd = jnp.einsum("lgcd,gh->lgchd", pool_w, eye).reshape(depth, D_A, D_A).astype(BF16)
    tri = jnp.tril(jnp.ones((SGU_CHUNK, SGU_CHUNK), bool))
    sw = jnp.where(tri, sgu_w, 0.0)
    sw_b = sw.astype(BF16)
    hd_b = D_B // N_HEADS_B
    sb_exp = jnp.repeat(jnp.swapaxes(sgu_b, 1, 2), hd_b, axis=2)
    sw4 = jnp.repeat(jnp.transpose(sw[:, :, :n_new, :n_new], (0, 2, 3, 1)), hd_b, axis=3)
    sw4 = sw4.reshape(depth, n_new * n_new, D_B)
    w_out_b = w_out.astype(BF16)
    wgd, wud, wdd = (w.astype(BF16) for w in (w_gate_dense, w_up_dense, w_down_dense))
    wgm, wum, wdm = (w.astype(BF16) for w in (w_gate_moe, w_up_moe, w_down_moe))
    router_pad = jnp.pad(router, ((0, 0), (0, 0), (0, LANES - N_EXPERTS)))
    lf_flat = cache_logf.reshape(depth, n_phys, page * N_HEADS_C // LANES, LANES)
    state = state_pool.reshape(depth, db, POOL_STATE * D_A)
    row = lambda v: v[None, :]

    xp = x_prompt.reshape(tp, D_MODEL)
    xs = x_sample.reshape(db, n_new * D_MODEL)
    pool_p, k_p, v_p, lf_p = [], [], [], []
    pool_s, sgu_s, k_s, v_s, lf_s = [], [], [], [], []
    for l in range(depth):
        gm, gf = row(norm_mix[l]), row(norm_ffn[l])
        a, u, vb, q, k, va, lf, f = _in_proj(xp, gm, w_in_b[l], bf_pad[l], seq_len=seq_len)
        yc = _attn_prompt(q, k, va, f, batch=batch, seq_len=seq_len)
        xp = _mix_prompt(xp, a, u, vb, yc, wbd[l], row(pool_scale[l]), sw_b[l], sb_exp[l],
                         row(group_gain[l]), w_out_b[l], seq_len=seq_len)
        pool_p.append(a.reshape(batch, seq_len, D_A)[:, -POOL_STATE:])
        k_p.append(k)
        v_p.append(va)
        lf_p.append(lf)
        a, u, vb, q, k, va, lf = _in_proj(xs.reshape(db * n_new, D_MODEL), gm, w_in_b[l], bf_pad[l])
        heads = lambda z: z.reshape(db, n_new, N_HEADS_C, HEAD_DIM)
        k4, v4 = heads(k), heads(va)
        yc = _attn_sample(l, heads(q), k4, v4, lf.reshape(db, 1, n_new * N_HEADS_C), cache_k, cache_v,
                          lf_flat, page_table)
        lanes = lambda z: z.reshape(db, -1)
        xs, pool = _mix_sample(l, xs, lanes(a), state, lanes(u), lanes(vb), lanes(yc), wbd[l],
                               row(pool_scale[l]), sw4[l], sb_exp[l, :n_new], row(group_gain[l]),
                               w_out_b[l], n_new=n_new)
        pool_s.append(pool)
        sgu_s.append(vb)
        k_s.append(k4)
        v_s.append(v4)
        lf_s.append(lf)
        i = l // 2
        last = l == depth - 1
        xs = xs.reshape(db * n_new, D_MODEL)
        if l % 2 == 0:
            xp = _ffn(xp, gf, wgd, wud, wdd, i, tm=_tile(tp, 512))
            xs = _ffn(xs, gf, wgd, wud, wdd, i, tm=_tile(db * n_new, 512))
            if last:
                xp, xs = _final_norm(xp, row(norm_final)), _final_norm(xs, row(norm_final))
        else:
            gfin = row(norm_final)
            xp = _moe(xp, gf, router_pad[i], wgm, wum, wdm, i, gfin, tm=_tile(tp, 512), final=last)
            xs = _moe(xs, gf, router_pad[i], wgm, wum, wdm, i, gfin, tm=_tile(db * n_new, 128), final=last)
        xs = xs.reshape(db, n_new * D_MODEL)
    st = lambda zs, shape: jnp.stack(zs, 0).reshape((depth,) + shape)
    return (xp.reshape(batch, seq_len, D_MODEL), xs.reshape(db, n_new, D_MODEL),
            st(pool_p, (batch, POOL_STATE, D_A)),
            st(k_p, (batch, seq_len, N_HEADS_C, HEAD_DIM)), st(v_p, (batch, seq_len, N_HEADS_C, HEAD_DIM)),
            st(lf_p, (batch, seq_len, N_HEADS_C)),
            st(pool_s, (db, POOL_STATE, D_A)), st(sgu_s, (db, n_new, D_B)),
            st(k_s, (db, n_new, N_HEADS_C, HEAD_DIM)), st(v_s, (db, n_new, N_HEADS_C, HEAD_DIM)),
            st(lf_s, (db, n_new, N_HEADS_C)))
```

```python
import functools

import jax
import jax.numpy as jnp
import numpy as np
from jax import lax
from jax.experimental import pallas as pl
from jax.experimental.pallas import tpu as pltpu

F32 = jnp.float32
BF16 = jnp.bfloat16
HIGHEST = lax.Precision.HIGHEST

D_MODEL = 1024
D_A = 256
D_B = 256
D_C = 512
HEAD_DIM = 64
N_HEADS_C = 8
N_HEADS_B = 4
POOL_WINDOWS = (2, 4, 8, 16)
POOL_GROUP_DIM = 64
POOL_STATE = 15
SGU_CHUNK = 128
D_FF = 2816
N_EXPERTS = 8
TOP_K = 2
EPS = 1e-6
NEG_INF = -1e30
ATTN_SCALE = HEAD_DIM ** -0.5
LOG2E = 1.4426950408889634

LANES = 128
SUBLANES = 8
HALO_ROWS = 16
OFF_U = D_A
OFF_V = OFF_U + D_B
OFF_Q = OFF_V + D_B
OFF_K = OFF_Q + D_C
OFF_VA = OFF_K + D_C
OFF_F = OFF_VA + D_C
D_IN_PAD = OFF_F + LANES
FF_CHUNK = D_FF // 2
F_SPLIT = 3
DMA_UNROLL = 8
N_DMA_PRIORITIES = 2
ATTN_TQ = 512
ATTN_TK = 512
V7X_VMEM_BYTES = 64 * 1024 * 1024
VMEM_LIMIT = (V7X_VMEM_BYTES * 7) // 8

NT_DIMS = (((1,), (1,)), ((), ()))


def _params(*sem):
    return pltpu.CompilerParams(dimension_semantics=sem, vmem_limit_bytes=VMEM_LIMIT)


def _rms(x, g):
    return x * lax.rsqrt(jnp.mean(x * x, axis=-1, keepdims=True) + EPS) * g


def _log_sigmoid(x):
    return jnp.minimum(x, 0.0) - jnp.log1p(jnp.exp(-jnp.abs(x)))


def _silu(x):
    return x / (1.0 + jnp.exp(-x))


def _div_pow2(x, d):
    assert d & (d - 1) == 0
    return lax.shift_right_logical(x, d.bit_length() - 1)


def _tile(n, pref):
    t = min(n, pref)
    while n % t:
        t //= 2
    return t


def _in_proj_kernel(x_ref, g_ref, w_ref, bf_ref, a_ref, u_ref, v_ref, q_ref, k_ref, va_ref,
                    lf_ref, *cum, tiles_per_seq):
    h = _rms(x_ref[...], g_ref[...]).astype(BF16)
    z = jnp.dot(h, w_ref[...], preferred_element_type=F32)
    a_ref[...] = z[:, :OFF_U]
    u_ref[...] = jax.nn.gelu(z[:, OFF_U:OFF_V])
    v_ref[...] = jax.nn.gelu(z[:, OFF_V:OFF_Q])
    q_ref[...] = z[:, OFF_Q:OFF_K]
    k_ref[...] = z[:, OFF_K:OFF_VA]
    va_ref[...] = z[:, OFF_VA:OFF_F]
    lf = _log_sigmoid(z[:, OFF_F:] + bf_ref[...])
    lf_ref[...] = lf[:, :N_HEADS_C]
    if cum:
        f_ref, carry_ref = cum
        tm = lf.shape[0]

        @pl.when(pl.program_id(0) % tiles_per_seq == 0)
        def _():
            carry_ref[...] = jnp.zeros_like(carry_ref)

        lane = lax.broadcasted_iota(jnp.int32, (1, LANES), 1)
        lf = jnp.where(lane < N_HEADS_C, lf, 0.0)
        r = lax.broadcasted_iota(jnp.int32, (LANES, LANES), 0)
        c = lax.broadcasted_iota(jnp.int32, (LANES, LANES), 1)
        tril = (r >= c).astype(F32)
        for sb in range(tm // LANES):
            rows = slice(sb * LANES, (sb + 1) * LANES)
            fb = jnp.dot(tril, lf[rows, :], precision=HIGHEST, preferred_element_type=F32) + carry_ref[...]
            f_ref[rows, :] = fb[:, :N_HEADS_C]
            carry_ref[...] = fb[LANES - 1:LANES, :]


def _in_proj(x, g, w, bf, *, seq_len=None):
    t = x.shape[0]
    tm = _tile(seq_len if seq_len else t, 512)
    row = lambda n: pl.BlockSpec((tm, n), lambda i: (i, 0))
    full = lambda s: pl.BlockSpec(s, lambda i: (0, 0))
    widths = (D_A, D_B, D_B, D_C, D_C, D_C, N_HEADS_C) + ((N_HEADS_C,) if seq_len else ())
    return pl.pallas_call(
        functools.partial(_in_proj_kernel, tiles_per_seq=(seq_len // tm) if seq_len else 1),
        grid=(t // tm,),
        in_specs=[row(D_MODEL), full((1, D_MODEL)), full((D_MODEL, D_IN_PAD)), full((1, LANES))],
        out_specs=[row(n) for n in widths],
        out_shape=[jax.ShapeDtypeStruct((t, n), F32) for n in widths],
        scratch_shapes=[pltpu.VMEM((1, LANES), F32)] if seq_len else [],
        compiler_params=_params("arbitrary"),
        name="in_proj",
    )(x, g, w, bf)


def _split_bf16(x):
    pieces = []
    for _ in range(F_SPLIT - 1):
        p = x.astype(BF16).astype(F32)
        pieces.append(p)
        x = x - p
    return pieces + [x]


def _attn_prompt_kernel(q_ref, k_ref, v_ref, f_ref, o_ref, kop_ref, vop_ref, acc_ref, *, tq, tk):
    g = pl.program_id(1)
    i = pl.program_id(2)
    per_q = tq // tk
    n_blocks = k_ref.shape[0] // tk
    lane = lax.broadcasted_iota(jnp.int32, (1, LANES), 1)
    lane8 = lax.broadcasted_iota(jnp.int32, (1, N_HEADS_C), 1)
    own = (lane < HEAD_DIM, lane >= HEAD_DIM)
    spare = (HEAD_DIM, 0)

    def head_col(x8, hh):
        return jnp.sum(jnp.where(lane8 == 2 * g + hh, x8, 0.0), axis=1, keepdims=True)

    def block_rows(j):
        return pl.ds(pl.multiple_of(j * tk, tk), tk)

    @pl.when(i == 0)
    def _():
        def prepare(c, _):
            rows = block_rows(c)
            kf = k_ref[rows, :]
            vf = v_ref[rows, :]
            rel = (f_ref[rows, :] - f_ref[pl.ds(pl.multiple_of(c * tk, tk), 1), :]) * LOG2E
            for hh in range(2):
                aug = 0.0
                for n, piece in enumerate(_split_bf16(head_col(rel, hh))):
                    aug = jnp.where(lane == spare[hh] + n, -piece, aug)
                kop_ref[hh, rows, :] = jnp.where(own[hh], kf, aug).astype(BF16)
                ones = jnp.where(lane == spare[hh], 1.0, 0.0)
                vop_ref[hh, rows, :] = jnp.where(own[hh], vf, ones).astype(BF16)
            return 0

        lax.fori_loop(0, n_blocks, prepare, 0)

    q = q_ref[...] * (ATTN_SCALE * LOG2E)
    q_heads = []
    for hh in range(2):
        ones = jnp.where(jnp.logical_and(lane >= spare[hh], lane < spare[hh] + F_SPLIT), 1.0, 0.0)
        q_heads.append(jnp.where(own[hh], q, ones).astype(BF16))
    fq8 = f_ref[pl.ds(pl.multiple_of(i * tq, tq), tq), :]
    fq = [head_col(fq8, hh) for hh in range(2)]
    row = lax.broadcasted_iota(jnp.int32, (tq, tk), 0)
    col = lax.broadcasted_iota(jnp.int32, (tq, tk), 1)
    acc_ref[...] = jnp.zeros_like(acc_ref)

    def scores(j):
        return tuple(lax.dot_general(q_heads[hh], kop_ref[hh, block_rows(j), :], NT_DIMS,
                                     preferred_element_type=F32) for hh in range(2))

    def consume(j, s_pair, m, first_col):
        start8 = f_ref[pl.ds(pl.multiple_of(j * tk, tk), 1), :]
        m_out, probs, alphas = [], [], []
        for hh in range(2):
            s = s_pair[hh] if first_col is None else jnp.where(col + first_col <= row, s_pair[hh], NEG_INF)
            fqj = (fq[hh] - head_col(start8, hh)) * LOG2E
            m_new = jnp.maximum(m[hh], jnp.max(s, axis=1, keepdims=True) + fqj)
            probs.append(jnp.exp2(s - (m_new - fqj)).astype(BF16))
            alphas.append(jnp.exp2(m[hh] - m_new))
            m_out.append(m_new)
        pv = [jnp.dot(probs[hh], vop_ref[hh, block_rows(j), :], preferred_element_type=F32) for hh in range(2)]
        for hh in range(2):
            acc_ref[hh] = alphas[hh] * acc_ref[hh] + pv[hh]
        return tuple(m_out)

    def step(j, carry):
        s_cur, m = carry
        s_next = scores(j + 1)
        return s_next, consume(j, s_cur, m, None)

    first = i * per_q
    m0 = (jnp.full((tq, 1), NEG_INF, F32),) * 2
    s_cur, m = lax.fori_loop(0, first, step, (scores(0), m0))
    for d in range(per_q):
        s_next = scores(first + d + 1) if d + 1 < per_q else None
        m = consume(first + d, s_cur, m, d * tk)
        s_cur = s_next
    acc0, acc1 = acc_ref[0], acc_ref[1]
    o_ref[...] = jnp.where(own[0], acc0 / acc0[:, spare[0]:spare[0] + 1], acc1 / acc1[:, spare[1]:spare[1] + 1])


def _attn_prompt(q, k, v, f, *, batch, seq_len):
    tq = _tile(seq_len, ATTN_TQ)
    tk = _tile(tq, ATTN_TK)
    nq = seq_len // tq
    qspec = pl.BlockSpec((tq, LANES), lambda b, g, i: (b * nq + i, g))
    kvspec = pl.BlockSpec((seq_len, LANES), lambda b, g, i: (b, g))
    return pl.pallas_call(
        functools.partial(_attn_prompt_kernel, tq=tq, tk=tk),
        grid=(batch, D_C // LANES, nq),
        in_specs=[qspec, kvspec, kvspec, pl.BlockSpec((seq_len, N_HEADS_C), lambda b, g, i: (b, 0))],
        out_specs=qspec,
        out_shape=jax.ShapeDtypeStruct(q.shape, F32),
        scratch_shapes=[pltpu.VMEM((2, seq_len, LANES), BF16), pltpu.VMEM((2, seq_len, LANES), BF16),
                        pltpu.VMEM((2, tq, LANES), F32)],
        compiler_params=_params("arbitrary", "arbitrary", "arbitrary"),
        name="attn_prompt",
    )(q, k, v, f)


def _forget_consts(n_flat):
    s = np.arange(LANES)
    after = s[:, None] > s[None, :]
    r = np.arange(n_flat)
    later_page = np.logical_and(r[None, :] // N_HEADS_C > r[:, None] // N_HEADS_C,
                                r[None, :] % N_HEADS_C == r[:, None] % N_HEADS_C)
    return after.astype(np.float32), later_page.astype(np.float32)


def _page_copies(layer, seq, slot, pt_ref, caches, bufs, sems, n_pages):
    (ck, cv, clf), (kbuf, vbuf, lfbuf) = caches, bufs
    copies = []
    for p in range(n_pages):
        phys = pt_ref[seq * n_pages + p]
        copies.append(pltpu.make_async_copy(ck.at[layer, phys], kbuf.at[slot, p], sems.at[0, slot]))
        copies.append(pltpu.make_async_copy(cv.at[layer, phys], vbuf.at[slot, p], sems.at[1, slot]))
        copies.append(pltpu.make_async_copy(clf.at[layer, phys], lfbuf.at[slot, pl.ds(p * N_HEADS_C, N_HEADS_C)],
                                            sems.at[2, slot]))
    return copies


def _attn_sample_seq(slot, local, q_ref, kn_ref, vn_ref, lfn_ref, after_ref, later_ref, o_ref,
                     kbuf, vbuf, lfbuf, *, n_pages, n_new):
    n_rows = n_new * N_HEADS_C
    kbuf[slot, n_pages, :, 0:n_new] = kn_ref[local]
    vbuf[slot, n_pages, :, 0:n_new] = vn_ref[local]
    lfbuf[slot, n_pages * N_HEADS_C:(n_pages + 1) * N_HEADS_C, 0:n_new] = lfn_ref[local]

    x = lfbuf[slot]
    totals = jnp.broadcast_to(jnp.sum(x, axis=1, keepdims=True), x.shape)
    g_all = (jnp.dot(x, after_ref[...], precision=HIGHEST, preferred_element_type=F32)
             + jnp.dot(later_ref[...], totals, precision=HIGHEST, preferred_element_type=F32))

    q = q_ref[local] * ATTN_SCALE
    r8 = lax.broadcasted_iota(jnp.int32, (N_HEADS_C, D_C), 0)
    c8 = lax.broadcasted_iota(jnp.int32, (N_HEADS_C, D_C), 1)
    head_mask = _div_pow2(c8, HEAD_DIM) == r8
    qbd = jnp.concatenate(
        [jnp.where(head_mask, jnp.broadcast_to(q[t:t + 1, :], (N_HEADS_C, D_C)), 0.0) for t in range(n_new)],
        axis=0).astype(BF16)

    s_pages = []
    for p in range(n_pages + 1):
        g = g_all[p * N_HEADS_C:(p + 1) * N_HEADS_C, :]
        s = jnp.dot(qbd, kbuf[slot, p].astype(BF16), preferred_element_type=F32)
        s_pages.append(s + jnp.concatenate([g] * n_new, axis=0))
    key = lax.broadcasted_iota(jnp.int32, s_pages[0].shape, 1)
    qt = _div_pow2(lax.broadcasted_iota(jnp.int32, s_pages[0].shape, 0), N_HEADS_C)
    s_pages[n_pages] = jnp.where(key <= qt, s_pages[n_pages], NEG_INF)
    m = functools.reduce(jnp.maximum, [jnp.max(s, axis=1, keepdims=True) for s in s_pages])
    probs = [jnp.exp(s - m) for s in s_pages]
    denom = functools.reduce(jnp.add, [jnp.sum(p, axis=1, keepdims=True) for p in probs])
    out = None
    for p in range(n_pages + 1):
        d = lax.dot_general(probs[p].astype(BF16), vbuf[slot, p].astype(BF16), NT_DIMS,
                            preferred_element_type=F32)
        out = d if out is None else out + d
    out = out / denom
    for t in range(n_new):
        blk = out[t * N_HEADS_C:(t + 1) * N_HEADS_C, :]
        o_ref[local, t:t + 1, :] = jnp.sum(jnp.where(head_mask, blk, 0.0), axis=0, keepdims=True)


def _attn_sample_kernel(pt_ref, q_ref, kn_ref, vn_ref, lfn_ref, after_ref, later_ref, ck_hbm, cv_hbm, clf_hbm,
                        o_ref, kbuf, vbuf, lfbuf, sems, *, layer, n_pages, n_new):
    i = pl.program_id(0)
    n_steps = pl.num_programs(0)
    bufs = (kbuf, vbuf, lfbuf)
    copies = lambda seq, slot: _page_copies(layer, seq, slot, pt_ref, (ck_hbm, cv_hbm, clf_hbm), bufs,
                                            sems, n_pages)
    seq = functools.partial(_attn_sample_seq, q_ref=q_ref, kn_ref=kn_ref, vn_ref=vn_ref, lfn_ref=lfn_ref,
                            after_ref=after_ref, later_ref=later_ref, o_ref=o_ref, kbuf=kbuf, vbuf=vbuf,
                            lfbuf=lfbuf, n_pages=n_pages, n_new=n_new)

    @pl.when(i == 0)
    def _():
        for slot in range(2):
            kbuf[slot, n_pages] = jnp.zeros(kbuf.shape[2:], F32)
            vbuf[slot, n_pages] = jnp.zeros(vbuf.shape[2:], F32)
            lfbuf[slot, n_pages * N_HEADS_C:, :] = jnp.zeros((lfbuf.shape[1] - n_pages * N_HEADS_C, LANES), F32)
        for c in copies(0, 0):
            c.start()

    for c in copies(2 * i + 1, 1):
        c.start()
    for c in copies(2 * i, 0):
        c.wait()
    seq(0, 0)

    @pl.when(i + 1 < n_steps)
    def _():
        for c in copies(2 * i + 2, 0):
            c.start()

    for c in copies(2 * i + 1, 1):
        c.wait()
    seq(1, 1)


def _attn_sample(layer, q, k_new, v_new, lf_new, cache_k, cache_v, cache_lf, page_table):
    db, n_new, _ = q.shape
    n_pages = page_table.shape[1]
    page = cache_k.shape[3]
    assert db % 2 == 0 and page == LANES and n_new <= page
    flat_rows = -(-((n_pages + 1) * N_HEADS_C) // LANES) * LANES
    after, later = _forget_consts(flat_rows)
    per_seq = lambda shape: pl.BlockSpec((2,) + shape, lambda b, pt: (b, 0, 0))
    const = lambda n: pl.BlockSpec((n, n), lambda b, pt: (0, 0))
    hbm = pl.BlockSpec(memory_space=pl.ANY)
    grid_spec = pltpu.PrefetchScalarGridSpec(
        num_scalar_prefetch=1,
        grid=(db // 2,),
        in_specs=[per_seq((n_new, D_C)), per_seq((D_C, n_new)), per_seq((D_C, n_new)),
                  per_seq((N_HEADS_C, n_new)), const(LANES), const(flat_rows), hbm, hbm, hbm],
        out_specs=per_seq((n_new, D_C)),
        scratch_shapes=[pltpu.VMEM((2, n_pages + 1, D_C, page), F32),
                        pltpu.VMEM((2, n_pages + 1, D_C, page), F32),
                        pltpu.VMEM((2, flat_rows, LANES), F32),
                        pltpu.SemaphoreType.DMA((3, 2))],
    )
    return pl.pallas_call(
        functools.partial(_attn_sample_kernel, layer=layer, n_pages=n_pages, n_new=n_new),
        grid_spec=grid_spec,
        out_shape=jax.ShapeDtypeStruct(q.shape, F32),
        compiler_params=_params("arbitrary"),
        name="attn_sample",
    )(page_table.reshape(-1), q, k_new, v_new, lf_new, jnp.asarray(after), jnp.asarray(later),
      cache_k, cache_v, cache_lf)


def _pool_select(sums, cnt, a_new, wbd, scale):
    lane = _div_pow2(lax.broadcasted_iota(jnp.int32, (1, D_A), 1), POOL_GROUP_DIM)
    sel = jnp.where(lane == 0, sums[0], jnp.where(lane == 1, sums[1], jnp.where(lane == 2, sums[2], sums[3])))
    d = sel / cnt - a_new
    return jnp.dot(d.astype(BF16), wbd, preferred_element_type=F32) * scale


def _merge(ya, yb, yc, gain, w_out):
    y = jnp.concatenate([_rms(ya, gain[:, :D_A]), _rms(yb, gain[:, D_A:D_A + D_B]),
                         _rms(yc, gain[:, D_A + D_B:])], axis=-1)
    return jnp.dot(y.astype(BF16), w_out, preferred_element_type=F32)


def _window_lanes():
    lane = _div_pow2(lax.broadcasted_iota(jnp.int32, (1, D_A), 1), POOL_GROUP_DIM)
    return jnp.where(lane == 0, 2.0, jnp.where(lane == 1, 4.0, jnp.where(lane == 2, 8.0, 16.0)))


def _mix_prompt_kernel(x_ref, a_ref, halo_ref, u_ref, v_ref, yc_ref, wbd_ref, ps_ref, sw_ref, sb_ref,
                       gain_ref, wo_ref, o_ref, ext_ref, *, tiles_per_seq):
    ts = a_ref.shape[0]
    it = pl.program_id(0) % tiles_per_seq
    a = a_ref[...]
    ext_ref[0:HALO_ROWS, :] = jnp.where(it == 0, 0.0, halo_ref[...])
    ext_ref[HALO_ROWS:, :] = a
    sums, acc = [], None
    for j in range(max(POOL_WINDOWS)):
        sl = ext_ref[HALO_ROWS - j:HALO_ROWS - j + ts, :]
        acc = sl if acc is None else acc + sl
        if j + 1 in POOL_WINDOWS:
            sums.append(acc)
    pos1 = (it * ts + 1 + lax.broadcasted_iota(jnp.int32, (ts, 1), 0)).astype(F32)
    cnt = jnp.minimum(pos1, _window_lanes())
    ya = _pool_select(sums, cnt, a, wbd_ref[...], ps_ref[...])
    head = _div_pow2(lax.broadcasted_iota(jnp.int32, (1, D_B), 1), D_B // N_HEADS_B)
    ybs = []
    for c in range(ts // SGU_CHUNK):
        rows = slice(c * SGU_CHUNK, (c + 1) * SGU_CHUNK)
        vc = v_ref[rows, :].astype(BF16)
        zc = jnp.zeros_like(vc)
        s = sb_ref[...]
        for h in range(N_HEADS_B):
            s = s + jnp.dot(sw_ref[h], jnp.where(head == h, vc, zc), preferred_element_type=F32)
        ybs.append(u_ref[rows, :] * s)
    yb = jnp.concatenate(ybs, axis=0)
    o_ref[...] = x_ref[...] + _merge(ya, yb, yc_ref[...], gain_ref[...], wo_ref[...])


def _mix_prompt(x, a, u, v, yc, wbd, ps, sw, sb, gain, wo, *, seq_len):
    t = x.shape[0]
    ts = _tile(seq_len, 512)
    hpt = ts // HALO_ROWS
    row = lambda n: pl.BlockSpec((ts, n), lambda i: (i, 0))
    full = lambda s: pl.BlockSpec(s, lambda i: (0,) * len(s))
    return pl.pallas_call(
        functools.partial(_mix_prompt_kernel, tiles_per_seq=seq_len // ts),
        grid=(t // ts,),
        in_specs=[row(D_MODEL), row(D_A),
                  pl.BlockSpec((HALO_ROWS, D_A), lambda i: (jnp.maximum(i * hpt - 1, 0), 0)),
                  row(D_B), row(D_B), row(D_C), full((D_A, D_A)), full((1, D_A)),
                  full((N_HEADS_B, SGU_CHUNK, SGU_CHUNK)), full((SGU_CHUNK, D_B)),
                  full((1, D_MODEL)), full((D_MODEL, D_MODEL))],
        out_specs=row(D_MODEL),
        out_shape=jax.ShapeDtypeStruct(x.shape, F32),
        scratch_shapes=[pltpu.VMEM((ts + HALO_ROWS, D_A), F32)],
        compiler_params=_params("arbitrary"),
        name="mix_prompt",
    )(x, a, a, u, v, yc, wbd, ps, sw, sb, gain, wo)


def _mix_sample_kernel(x_ref, a_ref, st_ref, u_ref, v_ref, yc_ref, wbd_ref, ps_ref, sw_ref, sb_ref,
                       gain_ref, wo_ref, o_ref, pool_ref, *, n_new):
    def ext(p):
        if p < POOL_STATE:
            return st_ref[:, p * D_A:(p + 1) * D_A]
        return a_ref[:, (p - POOL_STATE) * D_A:(p - POOL_STATE + 1) * D_A]

    for t in range(n_new):
        j = POOL_STATE + t
        sums, acc = [], None
        for w in range(max(POOL_WINDOWS)):
            acc = ext(j - w) if acc is None else acc + ext(j - w)
            if w + 1 in POOL_WINDOWS:
                sums.append(acc)
        ya = _pool_select(sums, _window_lanes(), ext(j), wbd_ref[...], ps_ref[...])
        s = sb_ref[t:t + 1, :]
        for k in range(t + 1):
            s = s + sw_ref[t * n_new + k:t * n_new + k + 1, :] * v_ref[:, k * D_B:(k + 1) * D_B]
        yb = u_ref[:, t * D_B:(t + 1) * D_B] * s
        yc = yc_ref[:, t * D_C:(t + 1) * D_C]
        cols = slice(t * D_MODEL, (t + 1) * D_MODEL)
        o_ref[:, cols] = x_ref[:, cols] + _merge(ya, yb, yc, gain_ref[...], wo_ref[...])
    keep = POOL_STATE - n_new
    pool_ref[:, :keep * D_A] = st_ref[:, n_new * D_A:]
    pool_ref[:, keep * D_A:] = a_ref[...]


def _mix_sample(layer, x, a, state, u, v, yc, wbd, ps, sw4, sb4, gain, wo, *, n_new):
    db = x.shape[0]
    full = lambda s: pl.BlockSpec(s, lambda i: (0,) * len(s))
    return pl.pallas_call(
        functools.partial(_mix_sample_kernel, n_new=n_new),
        grid=(1,),
        in_specs=[full(x.shape), full(a.shape),
                  pl.BlockSpec((None, db, POOL_STATE * D_A), lambda i: (layer, 0, 0)),
                  full(u.shape), full(v.shape), full(yc.shape), full((D_A, D_A)), full((1, D_A)),
                  full(sw4.shape), full(sb4.shape), full((1, D_MODEL)), full((D_MODEL, D_MODEL))],
        out_specs=[full(x.shape), full((db, POOL_STATE * D_A))],
        out_shape=[jax.ShapeDtypeStruct(x.shape, F32), jax.ShapeDtypeStruct((db, POOL_STATE * D_A), F32)],
        compiler_params=_params("arbitrary"),
        name="mix_sample",
    )(x, a, state, u, v, yc, wbd, ps, sw4, sb4, gain, wo)


def _ffn_kernel(te_ref, nt_ref, x_ref, g_ref, wg_ref, wu_ref, wd_ref, o_ref, h_ref, acc_ref, *, routed):
    del te_ref
    i = pl.program_id(0)
    j = pl.program_id(1)
    live = i < nt_ref[0]
    last = j == pl.num_programs(1) - 1

    @pl.when(jnp.logical_and(live, j == 0))
    def _():
        h_ref[...] = _rms(x_ref[...], g_ref[...]).astype(BF16)
        acc_ref[...] = jnp.zeros_like(acc_ref)

    @pl.when(live)
    def _():
        h = h_ref[...]
        gate = jnp.dot(h, wg_ref[...], preferred_element_type=F32)
        up = jnp.dot(h, wu_ref[...], preferred_element_type=F32)
        act = (_silu(gate) * up).astype(BF16)
        acc_ref[...] += jnp.dot(act, wd_ref[...], preferred_element_type=F32)

    @pl.when(jnp.logical_and(live, last))
    def _():
        o_ref[...] = acc_ref[...] if routed else x_ref[...] + acc_ref[...]

    @pl.when(jnp.logical_and(jnp.logical_not(live), last))
    def _():
        o_ref[...] = jnp.zeros_like(o_ref)


def _ffn(x, g, wg, wu, wd, li, *, tile_expert=None, n_live=None, tm):
    routed = tile_expert is not None
    r = x.shape[0]
    n_tiles = r // tm
    if not routed:
        tile_expert = jnp.zeros((n_tiles,), jnp.int32)
        n_live = jnp.full((1,), n_tiles, jnp.int32)
        wspec_in = pl.BlockSpec((None, D_MODEL, FF_CHUNK), lambda i, j, te, nt: (li, 0, j))
        wspec_out = pl.BlockSpec((None, FF_CHUNK, D_MODEL), lambda i, j, te, nt: (li, j, 0))
    else:
        wspec_in = pl.BlockSpec((None, None, D_MODEL, FF_CHUNK), lambda i, j, te, nt: (li, te[i], 0, j))
        wspec_out = pl.BlockSpec((None, None, FF_CHUNK, D_MODEL), lambda i, j, te, nt: (li, te[i], j, 0))
    row = pl.BlockSpec((tm, D_MODEL), lambda i, j, te, nt: (i, 0))
    grid_spec = pltpu.PrefetchScalarGridSpec(
        num_scalar_prefetch=2,
        grid=(n_tiles, D_FF // FF_CHUNK),
        in_specs=[row, pl.BlockSpec((1, D_MODEL), lambda i, j, te, nt: (0, 0)), wspec_in, wspec_in, wspec_out],
        out_specs=row,
        scratch_shapes=[pltpu.VMEM((tm, D_MODEL), BF16), pltpu.VMEM((tm, D_MODEL), F32)],
    )
    return pl.pallas_call(
        functools.partial(_ffn_kernel, routed=routed),
        grid_spec=grid_spec,
        out_shape=jax.ShapeDtypeStruct((r, D_MODEL), F32),
        compiler_params=_params("arbitrary", "arbitrary"),
        name="ffn_routed" if routed else "ffn_dense",
    )(tile_expert, n_live, x, g, wg, wu, wd)


def _route_kernel(x_ref, g_ref, wr_ref, idx_ref, gate_ref):
    h = _rms(x_ref[...], g_ref[...])
    logits = jnp.dot(h, wr_ref[...], precision=HIGHEST, preferred_element_type=F32)
    lane = lax.broadcasted_iota(jnp.int32, logits.shape, 1)
    lane_f = lane.astype(F32)
    logits = jnp.where(lane < N_EXPERTS, logits, -jnp.inf)
    m1 = jnp.max(logits, axis=1, keepdims=True)
    i1 = jnp.min(jnp.where(logits == m1, lane_f, float(LANES)), axis=1, keepdims=True)
    rest = jnp.where(lane_f == i1, -jnp.inf, logits)
    m2 = jnp.max(rest, axis=1, keepdims=True)
    i2 = jnp.min(jnp.where(rest == m2, lane_f, float(LANES)), axis=1, keepdims=True)
    e2 = jnp.exp(m2 - m1)
    den = 1.0 + e2
    idx_ref[...] = jnp.where(lane == 0, i1, jnp.where(lane == 1, i2, 0.0)).astype(jnp.int32)
    gate_ref[...] = jnp.where(lane == 0, 1.0 / den, jnp.where(lane == 1, e2 / den, 0.0))


def _route(x, g, wr):
    t = x.shape[0]
    tm = _tile(t, 512)
    row = lambda n: pl.BlockSpec((tm, n), lambda i: (i, 0))
    return pl.pallas_call(
        _route_kernel,
        grid=(t // tm,),
        in_specs=[row(D_MODEL), pl.BlockSpec((1, D_MODEL), lambda i: (0, 0)),
                  pl.BlockSpec((D_MODEL, LANES), lambda i: (0, 0))],
        out_specs=[row(LANES), row(LANES)],
        out_shape=[jax.ShapeDtypeStruct((t, LANES), jnp.int32), jax.ShapeDtypeStruct((t, LANES), F32)],
        compiler_params=_params("arbitrary"),
        name="route",
    )(x, g, wr)


def _dispatch_kernel(dst_ref, x_ref, init_hbm, o_hbm, sem):
    del init_hbm
    tm = x_ref.shape[0]
    base = pl.program_id(0) * tm

    def copy(r, k, slot):
        return pltpu.make_async_copy(x_ref.at[pl.ds(r, 1), :], o_hbm.at[pl.ds(slot, 1), :], sem)

    def start(r, _):
        for k in range(TOP_K):
            copy(r, k, dst_ref[(base + r) * TOP_K + k]).start(priority=k % N_DMA_PRIORITIES)
        return 0

    def wait(r, _):
        for k in range(TOP_K):
            copy(r, k, 0).wait()
        return 0

    lax.fori_loop(0, tm, start, 0, unroll=DMA_UNROLL)
    lax.fori_loop(0, tm, wait, 0, unroll=DMA_UNROLL)


def _dispatch(x, dst, rows):
    t = x.shape[0]
    tm = _tile(t, 256)
    hbm = pl.BlockSpec(memory_space=pl.ANY)
    grid_spec = pltpu.PrefetchScalarGridSpec(
        num_scalar_prefetch=1,
        grid=(t // tm,),
        in_specs=[pl.BlockSpec((tm, D_MODEL), lambda i, d: (i, 0)), hbm],
        out_specs=hbm,
        scratch_shapes=[pltpu.SemaphoreType.DMA(())],
    )
    return pl.pallas_call(
        _dispatch_kernel,
        grid_spec=grid_spec,
        out_shape=jax.ShapeDtypeStruct((rows, D_MODEL), F32),
        input_output_aliases={2: 0},
        compiler_params=_params("arbitrary"),
        name="dispatch",
    )(dst, x, jnp.zeros((rows, D_MODEL), F32))


def _combine_kernel(dst_ref, x_ref, gate_ref, y_hbm, g_ref, o_ref, buf_ref, sem, *, final):
    tm = x_ref.shape[0]
    base = pl.program_id(0) * tm

    def copy(r, k, slot):
        return pltpu.make_async_copy(y_hbm.at[pl.ds(slot, 1), :], buf_ref.at[k, pl.ds(r, 1), :], sem)

    def start(r, _):
        for k in range(TOP_K):
            copy(r, k, dst_ref[(base + r) * TOP_K + k]).start(priority=k % N_DMA_PRIORITIES)
        return 0

    def wait(r, _):
        for k in range(TOP_K):
            copy(r, k, 0).wait()
        return 0

    lax.fori_loop(0, tm, start, 0, unroll=DMA_UNROLL)
    lax.fori_loop(0, tm, wait, 0, unroll=DMA_UNROLL)
    gate = gate_ref[...]
    out = x_ref[...]
    for k in range(TOP_K):
        out = out + gate[:, k:k + 1] * buf_ref[k]
    o_ref[...] = _rms(out, g_ref[...]) if final else out


def _combine(x, gate, y, dst, g_final, *, final):
    t = x.shape[0]
    tm = _tile(t, 256)
    row = lambda n: pl.BlockSpec((tm, n), lambda i, d: (i, 0))
    grid_spec = pltpu.PrefetchScalarGridSpec(
        num_scalar_prefetch=1,
        grid=(t // tm,),
        in_specs=[row(D_MODEL), row(LANES), pl.BlockSpec(memory_space=pl.ANY),
                  pl.BlockSpec((1, D_MODEL), lambda i, d: (0, 0))],
        out_specs=row(D_MODEL),
        scratch_shapes=[pltpu.VMEM((TOP_K, tm, D_MODEL), F32), pltpu.SemaphoreType.DMA(())],
    )
    return pl.pallas_call(
        functools.partial(_combine_kernel, final=final),
        grid_spec=grid_spec,
        out_shape=jax.ShapeDtypeStruct(x.shape, F32),
        compiler_params=_params("arbitrary"),
        name="combine",
    )(dst, x, gate, y, g_final)


def _moe(x, g, wr, wg, wu, wd, li, g_final, *, tm, final):
    t = x.shape[0]
    idx, gate = _route(x, g, wr)
    e = idx[:, :TOP_K].reshape(-1)
    onehot = (e[:, None] == jnp.arange(N_EXPERTS, dtype=jnp.int32)[None, :]).astype(jnp.int32)
    rank = jnp.sum((jnp.cumsum(onehot, axis=0) - onehot) * onehot, axis=1)
    counts = jnp.sum(onehot, axis=0)
    padded = ((counts + tm - 1) // tm) * tm
    ends = jnp.cumsum(padded)
    dest = (jnp.sum(onehot * (ends - padded)[None, :], axis=1) + rank).astype(jnp.int32)
    n_tiles = (t * TOP_K) // tm + N_EXPERTS
    tile_start = jnp.arange(n_tiles, dtype=jnp.int32) * tm
    tile_expert = jnp.sum((ends[None, :] <= tile_start[:, None]).astype(jnp.int32), axis=1)
    tile_expert = jnp.minimum(tile_expert, N_EXPERTS - 1)
    n_live = (ends[-1:] // tm).astype(jnp.int32)
    xs = _dispatch(x, dest, n_tiles * tm)
    ys = _ffn(xs, g, wg, wu, wd, li, tile_expert=tile_expert, n_live=n_live, tm=tm)
    return _combine(x, gate, ys, dest, g_final, final=final)


def _final_norm_kernel(x_ref, g_ref, o_ref):
    o_ref[...] = _rms(x_ref[...], g_ref[...])


def _final_norm(x, g):
    t = x.shape[0]
    tm = _tile(t, 512)
    return pl.pallas_call(
        _final_norm_kernel,
        grid=(t // tm,),
        in_specs=[pl.BlockSpec((tm, D_MODEL), lambda i: (i, 0)), pl.BlockSpec((1, D_MODEL), lambda i: (0, 0))],
        out_specs=pl.BlockSpec((tm, D_MODEL), lambda i: (i, 0)),
        out_shape=jax.ShapeDtypeStruct(x.shape, F32),
        compiler_params=_params("arbitrary"),
        name="final_norm",
    )(x, g)


def kernel(x_prompt, x_sample, state_pool, cache_k, cache_v, cache_logf, page_table, norm_mix, w_in, b_f, pool_w, pool_scale, sgu_w, sgu_b, group_gain, w_out, norm_ffn, w_gate_dense, w_up_dense, w_down_dense, router, w_gate_moe, w_up_moe, w_down_moe, norm_final):
    depth = w_in.shape[0]
    batch, seq_len, _ = x_prompt.shape
    db, n_new, _ = x_sample.shape
    n_phys, page = cache_k.shape[1], cache_k.shape[2]
    tp = batch * seq_len

    w_in_b = jnp.pad(w_in, ((0, 0), (0, 0), (0, D_IN_PAD - w_in.shape[2]))).astype(BF16)
    bf_pad = jnp.pad(b_f, ((0, 0), (0, LANES - N_HEADS_C)))[:, None, :]
    eye = jnp.eye(len(POOL_WINDOWS), dtype=F32)
    wbd = jnp.einsum("lgcd,gh->lgchd", pool_w, eye).reshape(depth, D_A, D_A).astype(BF16)
    tri = jnp.tril(jnp.ones((SGU_CHUNK, SGU_CHUNK), bool))
    sw = jnp.where(tri, sgu_w, 0.0)
    sw_b = sw.astype(BF16)
    hd_b = D_B // N_HEADS_B
    sb_exp = jnp.repeat(jnp.swapaxes(sgu_b, 1, 2), hd_b, axis=2)
    sw4 = jnp.repeat(jnp.transpose(sw[:, :, :n_new, :n_new], (0, 2, 3, 1)), hd_b, axis=3)
    sw4 = sw4.reshape(depth, n_new * n_new, D_B)
    w_out_b = w_out.astype(BF16)
    wgd, wud, wdd = (w.astype(BF16) for w in (w_gate_dense, w_up_dense, w_down_dense))
    wgm, wum, wdm = (w.astype(BF16) for w in (w_gate_moe, w_up_moe, w_down_moe))
    router_pad = jnp.pad(router, ((0, 0), (0, 0), (0, LANES - N_EXPERTS)))
    ck_t = jnp.transpose(cache_k, (0, 1, 3, 4, 2)).reshape(depth, n_phys, D_C, page)
    cv_t = jnp.transpose(cache_v, (0, 1, 3, 4, 2)).reshape(depth, n_phys, D_C, page)
    clf_t = jnp.swapaxes(cache_logf, 2, 3)
    state = state_pool.reshape(depth, db, POOL_STATE * D_A)
    row = lambda v: v[None, :]

    xp = x_prompt.reshape(tp, D_MODEL)
    xs = x_sample.reshape(db, n_new * D_MODEL)
    pool_p, k_p, v_p, lf_p = [], [], [], []
    pool_s, sgu_s, k_s, v_s, lf_s = [], [], [], [], []
    for l in range(depth):
        gm, gf = row(norm_mix[l]), row(norm_ffn[l])
        a, u, vb, q, k, va, lf, f = _in_proj(xp, gm, w_in_b[l], bf_pad[l], seq_len=seq_len)
        yc = _attn_prompt(q, k, va, f, batch=batch, seq_len=seq_len)
        xp = _mix_prompt(xp, a, u, vb, yc, wbd[l], row(pool_scale[l]), sw_b[l], sb_exp[l],
                         row(group_gain[l]), w_out_b[l], seq_len=seq_len)
        pool_p.append(a.reshape(batch, seq_len, D_A)[:, -POOL_STATE:])
        k_p.append(k)
        v_p.append(va)
        lf_p.append(lf)
        a, u, vb, q, k, va, lf = _in_proj(xs.reshape(db * n_new, D_MODEL), gm, w_in_b[l], bf_pad[l])
        per_seq = lambda z: z.reshape(db, n_new, z.shape[-1])
        per_seq_t = lambda z: jnp.swapaxes(per_seq(z), 1, 2)
        yc = _attn_sample(l, per_seq(q), per_seq_t(k), per_seq_t(va), per_seq_t(lf), ck_t, cv_t, clf_t,
                          page_table)
        lanes = lambda z: z.reshape(db, -1)
        xs, pool = _mix_sample(l, xs, lanes(a), state, lanes(u), lanes(vb), lanes(yc), wbd[l],
                               row(pool_scale[l]), sw4[l], sb_exp[l, :n_new], row(group_gain[l]),
                               w_out_b[l], n_new=n_new)
        pool_s.append(pool)
        sgu_s.append(vb)
        k_s.append(k)
        v_s.append(va)
        lf_s.append(lf)
        i = l // 2
        last = l == depth - 1
        xs = xs.reshape(db * n_new, D_MODEL)
        if l % 2 == 0:
            xp = _ffn(xp, gf, wgd, wud, wdd, i, tm=_tile(tp, 512))
            xs = _ffn(xs, gf, wgd, wud, wdd, i, tm=_tile(db * n_new, 512))
            if last:
                xp, xs = _final_norm(xp, row(norm_final)), _final_norm(xs, row(norm_final))
        else:
            gfin = row(norm_final)
            xp = _moe(xp, gf, router_pad[i], wgm, wum, wdm, i, gfin, tm=_tile(tp, 512), final=last)
            xs = _moe(xs, gf, router_pad[i], wgm, wum, wdm, i, gfin, tm=_tile(db * n_new, 128), final=last)
        xs = xs.reshape(db, n_new * D_MODEL)
    st = lambda zs, shape: jnp.stack(zs, 0).reshape((depth,) + shape)
    return (xp.reshape(batch, seq_len, D_MODEL), xs.reshape(db, n_new, D_MODEL),
            st(pool_p, (batch, POOL_STATE, D_A)),
            st(k_p, (batch, seq_len, N_HEADS_C, HEAD_DIM)), st(v_p, (batch, seq_len, N_HEADS_C, HEAD_DIM)),
            st(lf_p, (batch, seq_len, N_HEADS_C)),
            st(pool_s, (db, POOL_STATE, D_A)), st(sgu_s, (db, n_new, D_B)),
            st(k_s, (db, n_new, N_HEADS_C, HEAD_DIM)), st(v_s, (db, n_new, N_HEADS_C, HEAD_DIM)),
            st(lf_s, (db, n_new, N_HEADS_C)))
```

```python
import functools

import jax
import jax.numpy as jnp
import numpy as np
from jax import lax
from jax.experimental import pallas as pl
from jax.experimental.pallas import tpu as pltpu

F32 = jnp.float32
BF16 = jnp.bfloat16
HIGHEST = lax.Precision.HIGHEST

D_MODEL = 1024
D_A = 256
D_B = 256
D_C = 512
HEAD_DIM = 64
N_HEADS_C = 8
N_HEADS_B = 4
POOL_WINDOWS = (2, 4, 8, 16)
POOL_GROUP_DIM = 64
POOL_STATE = 15
SGU_CHUNK = 128
D_FF = 2816
N_EXPERTS = 8
TOP_K = 2
EPS = 1e-6
NEG_INF = -1e30
ATTN_SCALE = HEAD_DIM ** -0.5
LOG2E = 1.4426950408889634

LANES = 128
SUBLANES = 8
HALO_ROWS = 16
OFF_U = D_A
OFF_V = OFF_U + D_B
OFF_Q = OFF_V + D_B
OFF_K = OFF_Q + D_C
OFF_VA = OFF_K + D_C
OFF_F = OFF_VA + D_C
D_IN_PAD = OFF_F + LANES
FF_CHUNK = D_FF // 2
F_SPLIT = 3
DMA_UNROLL = 8
N_DMA_PRIORITIES = 2
ATTN_TQ = 1024
ATTN_TK = 512
V7X_VMEM_BYTES = 64 * 1024 * 1024
VMEM_LIMIT = (V7X_VMEM_BYTES * 7) // 8

NT_DIMS = (((1,), (1,)), ((), ()))


def _params(*sem):
    return pltpu.CompilerParams(dimension_semantics=sem, vmem_limit_bytes=VMEM_LIMIT)


def _rms(x, g):
    return x * lax.rsqrt(jnp.mean(x * x, axis=-1, keepdims=True) + EPS) * g


def _log_sigmoid(x):
    return jnp.minimum(x, 0.0) - jnp.log1p(jnp.exp(-jnp.abs(x)))


def _silu(x):
    return x / (1.0 + jnp.exp(-x))


def _div_pow2(x, d):
    assert d & (d - 1) == 0
    return lax.shift_right_logical(x, d.bit_length() - 1)


def _tile(n, pref):
    t = min(n, pref)
    while n % t:
        t //= 2
    return t


def _in_proj_kernel(x_ref, g_ref, w_ref, bf_ref, a_ref, u_ref, v_ref, q_ref, k_ref, va_ref,
                    lf_ref, *cum, tiles_per_seq):
    h = _rms(x_ref[...], g_ref[...]).astype(BF16)
    z = jnp.dot(h, w_ref[...], preferred_element_type=F32)
    a_ref[...] = z[:, :OFF_U]
    u_ref[...] = jax.nn.gelu(z[:, OFF_U:OFF_V])
    v_ref[...] = jax.nn.gelu(z[:, OFF_V:OFF_Q])
    q_ref[...] = z[:, OFF_Q:OFF_K]
    k_ref[...] = z[:, OFF_K:OFF_VA]
    va_ref[...] = z[:, OFF_VA:OFF_F]
    lf = _log_sigmoid(z[:, OFF_F:] + bf_ref[...])
    lf_ref[...] = lf[:, :N_HEADS_C]
    if cum:
        f_ref, ft_ref, carry_ref = cum
        tm = lf.shape[0]

        @pl.when(pl.program_id(0) % tiles_per_seq == 0)
        def _():
            carry_ref[...] = jnp.zeros_like(carry_ref)

        lane = lax.broadcasted_iota(jnp.int32, (1, LANES), 1)
        lf = jnp.where(lane < N_HEADS_C, lf, 0.0)
        r = lax.broadcasted_iota(jnp.int32, (LANES, LANES), 0)
        c = lax.broadcasted_iota(jnp.int32, (LANES, LANES), 1)
        tril = (r >= c).astype(F32)
        for sb in range(tm // LANES):
            rows = slice(sb * LANES, (sb + 1) * LANES)
            fb = jnp.dot(tril, lf[rows, :], precision=HIGHEST, preferred_element_type=F32) + carry_ref[...]
            f_ref[rows, :] = fb[:, :N_HEADS_C]
            ft_ref[:, rows] = jnp.transpose(fb)[:N_HEADS_C, :]
            carry_ref[...] = fb[LANES - 1:LANES, :]


def _in_proj(x, g, w, bf, *, seq_len=None):
    t = x.shape[0]
    tm = _tile(seq_len if seq_len else t, 512)
    row = lambda n: pl.BlockSpec((tm, n), lambda i: (i, 0))
    full = lambda s: pl.BlockSpec(s, lambda i: (0, 0))
    widths = (D_A, D_B, D_B, D_C, D_C, D_C, N_HEADS_C) + ((N_HEADS_C,) if seq_len else ())
    out_specs = [row(n) for n in widths]
    out_shape = [jax.ShapeDtypeStruct((t, n), F32) for n in widths]
    if seq_len:
        out_specs.append(pl.BlockSpec((N_HEADS_C, tm), lambda i: (0, i)))
        out_shape.append(jax.ShapeDtypeStruct((N_HEADS_C, t), F32))
    return pl.pallas_call(
        functools.partial(_in_proj_kernel, tiles_per_seq=(seq_len // tm) if seq_len else 1),
        grid=(t // tm,),
        in_specs=[row(D_MODEL), full((1, D_MODEL)), full((D_MODEL, D_IN_PAD)), full((1, LANES))],
        out_specs=out_specs,
        out_shape=out_shape,
        scratch_shapes=[pltpu.VMEM((1, LANES), F32)] if seq_len else [],
        compiler_params=_params("arbitrary"),
        name="in_proj",
    )(x, g, w, bf)


def _split_bf16(x):
    pieces = []
    for _ in range(F_SPLIT - 1):
        p = x.astype(BF16).astype(F32)
        pieces.append(p)
        x = x - p
    return pieces + [x]


def _attn_prompt_kernel(q_ref, k_ref, v_ref, f_ref, ft_ref, o_ref, kop_ref, vop_ref, acc_ref, *, tq, tk):
    g = pl.program_id(1)
    i = pl.program_id(2)
    per_q = tq // tk
    n_blocks = k_ref.shape[0] // tk
    lane = lax.broadcasted_iota(jnp.int32, (1, LANES), 1)
    lane8 = lax.broadcasted_iota(jnp.int32, (1, N_HEADS_C), 1)
    chan = lax.broadcasted_iota(jnp.int32, (LANES, 1), 0)
    own = (lane < HEAD_DIM, lane >= HEAD_DIM)
    spare = (HEAD_DIM, 0)

    def head_col(x8, hh):
        return jnp.sum(jnp.where(lane8 == 2 * g + hh, x8, 0.0), axis=1, keepdims=True)

    def block_rows(j):
        return pl.ds(pl.multiple_of(j * tk, tk), tk)

    @pl.when(i == 0)
    def _():
        def prepare(c, _):
            rows = block_rows(c)
            kf = k_ref[rows, :]
            vf = v_ref[rows, :]
            rel = (f_ref[rows, :] - f_ref[pl.ds(pl.multiple_of(c * tk, tk), 1), :]) * LOG2E
            for hh in range(2):
                aug = 0.0
                for n, piece in enumerate(_split_bf16(head_col(rel, hh))):
                    aug = jnp.where(lane == spare[hh] + n, -piece, aug)
                kop_ref[hh, rows, :] = jnp.where(own[hh], kf, aug).astype(BF16)
                ones = jnp.where(lane == spare[hh], 1.0, 0.0)
                vop_ref[hh, c] = jnp.transpose(jnp.where(own[hh], vf, ones)).astype(BF16)
            return 0

        lax.fori_loop(0, n_blocks, prepare, 0)

    q = q_ref[...] * (ATTN_SCALE * LOG2E)
    q_heads = []
    for hh in range(2):
        ones = jnp.where(jnp.logical_and(lane >= spare[hh], lane < spare[hh] + F_SPLIT), 1.0, 0.0)
        q_heads.append(jnp.transpose(jnp.where(own[hh], q, ones)).astype(BF16))
    qcols = pl.ds(pl.multiple_of(i * tq, tq), tq)
    krow = lax.broadcasted_iota(jnp.int32, (tk, tq), 0)
    qcol = lax.broadcasted_iota(jnp.int32, (tk, tq), 1)
    acc_ref[...] = jnp.zeros_like(acc_ref)

    def scores(j):
        return tuple(jnp.dot(kop_ref[hh, block_rows(j), :], q_heads[hh], preferred_element_type=F32)
                     for hh in range(2))

    def consume(j, s_pair, m, first_key):
        k0 = pl.multiple_of(j * tk, tk)
        m_out, probs, alphas = [], [], []
        for hh in range(2):
            h = 2 * g + hh
            s = s_pair[hh] if first_key is None else jnp.where(krow + first_key <= qcol, s_pair[hh], NEG_INF)
            start = head_col(f_ref[pl.ds(k0, 1), :], hh)
            fqj = (ft_ref[pl.ds(h, 1), qcols] - start) * LOG2E
            m_new = jnp.maximum(m[hh], jnp.max(s, axis=0, keepdims=True) + fqj)
            probs.append(jnp.exp2(s - (m_new - fqj)).astype(BF16))
            alphas.append(jnp.exp2(m[hh] - m_new))
            m_out.append(m_new)
        pv = [jnp.dot(vop_ref[hh, j], probs[hh], preferred_element_type=F32) for hh in range(2)]
        for hh in range(2):
            acc_ref[hh] = alphas[hh] * acc_ref[hh] + pv[hh]
        return tuple(m_out)

    def step(j, m):
        return consume(j, scores(j), m, None)

    first = i * per_q
    m0 = (jnp.full((1, tq), NEG_INF, F32),) * 2
    m = lax.fori_loop(0, first, step, m0)
    for d in range(per_q):
        m = consume(first + d, scores(first + d), m, d * tk)
    acc0, acc1 = acc_ref[0], acc_ref[1]
    out_t = jnp.where(chan < HEAD_DIM, acc0 / acc0[spare[0]:spare[0] + 1, :], acc1 / acc1[spare[1]:spare[1] + 1, :])
    o_ref[...] = jnp.transpose(out_t)


def _attn_prompt(q, k, v, f, ft, *, batch, seq_len):
    tq = _tile(seq_len, ATTN_TQ)
    tk = _tile(tq, ATTN_TK)
    nq = seq_len // tq
    qspec = pl.BlockSpec((tq, LANES), lambda b, g, i: (b * nq + i, g))
    kvspec = pl.BlockSpec((seq_len, LANES), lambda b, g, i: (b, g))
    return pl.pallas_call(
        functools.partial(_attn_prompt_kernel, tq=tq, tk=tk),
        grid=(batch, D_C // LANES, nq),
        in_specs=[qspec, kvspec, kvspec, pl.BlockSpec((seq_len, N_HEADS_C), lambda b, g, i: (b, 0)),
                  pl.BlockSpec((N_HEADS_C, seq_len), lambda b, g, i: (0, b))],
        out_specs=qspec,
        out_shape=jax.ShapeDtypeStruct(q.shape, F32),
        scratch_shapes=[pltpu.VMEM((2, seq_len, LANES), BF16), pltpu.VMEM((2, seq_len // tk, LANES, tk), BF16),
                        pltpu.VMEM((2, LANES, tq), F32)],
        compiler_params=_params("arbitrary", "arbitrary", "arbitrary"),
        name="attn_prompt",
    )(q, k, v, f, ft)


def _forget_consts(n_flat):
    s = np.arange(LANES)
    after = s[:, None] > s[None, :]
    r = np.arange(n_flat)
    later_page = np.logical_and(r[None, :] // N_HEADS_C > r[:, None] // N_HEADS_C,
                                r[None, :] % N_HEADS_C == r[:, None] % N_HEADS_C)
    return after.astype(np.float32), later_page.astype(np.float32)


def _page_copies(layer, seq, slot, pt_ref, caches, bufs, sems, n_pages):
    (ck, cv, clf), (kbuf, vbuf, lfbuf) = caches, bufs
    copies = []
    for p in range(n_pages):
        phys = pt_ref[seq * n_pages + p]
        copies.append(pltpu.make_async_copy(ck.at[layer, phys], kbuf.at[slot, p], sems.at[0, slot]))
        copies.append(pltpu.make_async_copy(cv.at[layer, phys], vbuf.at[slot, p], sems.at[1, slot]))
        copies.append(pltpu.make_async_copy(clf.at[layer, phys], lfbuf.at[slot, pl.ds(p * N_HEADS_C, N_HEADS_C)],
                                            sems.at[2, slot]))
    return copies


def _attn_sample_seq(slot, local, q_ref, kn_ref, vn_ref, lfn_ref, after_ref, later_ref, o_ref,
                     kbuf, vbuf, lfbuf, *, n_pages, n_new):
    n_rows = n_new * N_HEADS_C
    kbuf[slot, n_pages, :, 0:n_new] = kn_ref[local]
    vbuf[slot, n_pages, :, 0:n_new] = vn_ref[local]
    lfbuf[slot, n_pages * N_HEADS_C:(n_pages + 1) * N_HEADS_C, 0:n_new] = lfn_ref[local]

    x = lfbuf[slot]
    totals = jnp.broadcast_to(jnp.sum(x, axis=1, keepdims=True), x.shape)
    g_all = (jnp.dot(x, after_ref[...], precision=HIGHEST, preferred_element_type=F32)
             + jnp.dot(later_ref[...], totals, precision=HIGHEST, preferred_element_type=F32))

    q = q_ref[local] * ATTN_SCALE
    r8 = lax.broadcasted_iota(jnp.int32, (N_HEADS_C, D_C), 0)
    c8 = lax.broadcasted_iota(jnp.int32, (N_HEADS_C, D_C), 1)
    head_mask = _div_pow2(c8, HEAD_DIM) == r8
    qbd = jnp.concatenate(
        [jnp.where(head_mask, jnp.broadcast_to(q[t:t + 1, :], (N_HEADS_C, D_C)), 0.0) for t in range(n_new)],
        axis=0).astype(BF16)

    s_pages = []
    for p in range(n_pages + 1):
        g = g_all[p * N_HEADS_C:(p + 1) * N_HEADS_C, :]
        s = jnp.dot(qbd, kbuf[slot, p].astype(BF16), preferred_element_type=F32)
        s_pages.append(s + jnp.concatenate([g] * n_new, axis=0))
    key = lax.broadcasted_iota(jnp.int32, s_pages[0].shape, 1)
    qt = _div_pow2(lax.broadcasted_iota(jnp.int32, s_pages[0].shape, 0), N_HEADS_C)
    s_pages[n_pages] = jnp.where(key <= qt, s_pages[n_pages], NEG_INF)
    m = functools.reduce(jnp.maximum, [jnp.max(s, axis=1, keepdims=True) for s in s_pages])
    probs = [jnp.exp(s - m) for s in s_pages]
    denom = functools.reduce(jnp.add, [jnp.sum(p, axis=1, keepdims=True) for p in probs])
    out = None
    for p in range(n_pages + 1):
        d = lax.dot_general(probs[p].astype(BF16), vbuf[slot, p].astype(BF16), NT_DIMS,
                            preferred_element_type=F32)
        out = d if out is None else out + d
    out = out / denom
    for t in range(n_new):
        blk = out[t * N_HEADS_C:(t + 1) * N_HEADS_C, :]
        o_ref[local, t:t + 1, :] = jnp.sum(jnp.where(head_mask, blk, 0.0), axis=0, keepdims=True)


def _attn_sample_kernel(pt_ref, q_ref, kn_ref, vn_ref, lfn_ref, after_ref, later_ref, ck_hbm, cv_hbm, clf_hbm,
                        o_ref, kbuf, vbuf, lfbuf, sems, *, layer, n_pages, n_new):
    i = pl.program_id(0)
    n_steps = pl.num_programs(0)
    bufs = (kbuf, vbuf, lfbuf)
    copies = lambda seq, slot: _page_copies(layer, seq, slot, pt_ref, (ck_hbm, cv_hbm, clf_hbm), bufs,
                                            sems, n_pages)
    seq = functools.partial(_attn_sample_seq, q_ref=q_ref, kn_ref=kn_ref, vn_ref=vn_ref, lfn_ref=lfn_ref,
                            after_ref=after_ref, later_ref=later_ref, o_ref=o_ref, kbuf=kbuf, vbuf=vbuf,
                            lfbuf=lfbuf, n_pages=n_pages, n_new=n_new)

    @pl.when(i == 0)
    def _():
        for slot in range(2):
            kbuf[slot, n_pages] = jnp.zeros(kbuf.shape[2:], F32)
            vbuf[slot, n_pages] = jnp.zeros(vbuf.shape[2:], F32)
            lfbuf[slot, n_pages * N_HEADS_C:, :] = jnp.zeros((lfbuf.shape[1] - n_pages * N_HEADS_C, LANES), F32)
        for c in copies(0, 0):
            c.start()

    for c in copies(2 * i + 1, 1):
        c.start()
    for c in copies(2 * i, 0):
        c.wait()
    seq(0, 0)

    @pl.when(i + 1 < n_steps)
    def _():
        for c in copies(2 * i + 2, 0):
            c.start()

    for c in copies(2 * i + 1, 1):
        c.wait()
    seq(1, 1)


def _attn_sample(layer, q, k_new, v_new, lf_new, cache_k, cache_v, cache_lf, page_table):
    db, n_new, _ = q.shape
    n_pages = page_table.shape[1]
    page = cache_k.shape[3]
    assert db % 2 == 0 and page == LANES and n_new <= page
    flat_rows = -(-((n_pages + 1) * N_HEADS_C) // LANES) * LANES
    after, later = _forget_consts(flat_rows)
    per_seq = lambda shape: pl.BlockSpec((2,) + shape, lambda b, pt: (b, 0, 0))
    const = lambda n: pl.BlockSpec((n, n), lambda b, pt: (0, 0))
    hbm = pl.BlockSpec(memory_space=pl.ANY)
    grid_spec = pltpu.PrefetchScalarGridSpec(
        num_scalar_prefetch=1,
        grid=(db // 2,),
        in_specs=[per_seq((n_new, D_C)), per_seq((D_C, n_new)), per_seq((D_C, n_new)),
                  per_seq((N_HEADS_C, n_new)), const(LANES), const(flat_rows), hbm, hbm, hbm],
        out_specs=per_seq((n_new, D_C)),
        scratch_shapes=[pltpu.VMEM((2, n_pages + 1, D_C, page), F32),
                        pltpu.VMEM((2, n_pages + 1, D_C, page), F32),
                        pltpu.VMEM((2, flat_rows, LANES), F32),
                        pltpu.SemaphoreType.DMA((3, 2))],
    )
    return pl.pallas_call(
        functools.partial(_attn_sample_kernel, layer=layer, n_pages=n_pages, n_new=n_new),
        grid_spec=grid_spec,
        out_shape=jax.ShapeDtypeStruct(q.shape, F32),
        compiler_params=_params("arbitrary"),
        name="attn_sample",
    )(page_table.reshape(-1), q, k_new, v_new, lf_new, jnp.asarray(after), jnp.asarray(later),
      cache_k, cache_v, cache_lf)


def _pool_select(sums, cnt, a_new, wbd, scale):
    lane = _div_pow2(lax.broadcasted_iota(jnp.int32, (1, D_A), 1), POOL_GROUP_DIM)
    sel = jnp.where(lane == 0, sums[0], jnp.where(lane == 1, sums[1], jnp.where(lane == 2, sums[2], sums[3])))
    d = sel / cnt - a_new
    return jnp.dot(d.astype(BF16), wbd, preferred_element_type=F32) * scale


def _merge(ya, yb, yc, gain, w_out):
    y = jnp.concatenate([_rms(ya, gain[:, :D_A]), _rms(yb, gain[:, D_A:D_A + D_B]),
                         _rms(yc, gain[:, D_A + D_B:])], axis=-1)
    return jnp.dot(y.astype(BF16), w_out, preferred_element_type=F32)


def _window_lanes():
    lane = _div_pow2(lax.broadcasted_iota(jnp.int32, (1, D_A), 1), POOL_GROUP_DIM)
    return jnp.where(lane == 0, 2.0, jnp.where(lane == 1, 4.0, jnp.where(lane == 2, 8.0, 16.0)))


def _mix_prompt_kernel(x_ref, a_ref, halo_ref, u_ref, v_ref, yc_ref, wbd_ref, ps_ref, sw_ref, sb_ref,
                       gain_ref, wo_ref, o_ref, ext_ref, *, tiles_per_seq):
    ts = a_ref.shape[0]
    it = pl.program_id(0) % tiles_per_seq
    a = a_ref[...]
    ext_ref[0:HALO_ROWS, :] = jnp.where(it == 0, 0.0, halo_ref[...])
    ext_ref[HALO_ROWS:, :] = a
    sums, acc = [], None
    for j in range(max(POOL_WINDOWS)):
        sl = ext_ref[HALO_ROWS - j:HALO_ROWS - j + ts, :]
        acc = sl if acc is None else acc + sl
        if j + 1 in POOL_WINDOWS:
            sums.append(acc)
    pos1 = (it * ts + 1 + lax.broadcasted_iota(jnp.int32, (ts, 1), 0)).astype(F32)
    cnt = jnp.minimum(pos1, _window_lanes())
    ya = _pool_select(sums, cnt, a, wbd_ref[...], ps_ref[...])
    head = _div_pow2(lax.broadcasted_iota(jnp.int32, (1, D_B), 1), D_B // N_HEADS_B)
    ybs = []
    for c in range(ts // SGU_CHUNK):
        rows = slice(c * SGU_CHUNK, (c + 1) * SGU_CHUNK)
        vc = v_ref[rows, :].astype(BF16)
        zc = jnp.zeros_like(vc)
        s = sb_ref[...]
        for h in range(N_HEADS_B):
            s = s + jnp.dot(sw_ref[h], jnp.where(head == h, vc, zc), preferred_element_type=F32)
        ybs.append(u_ref[rows, :] * s)
    yb = jnp.concatenate(ybs, axis=0)
    o_ref[...] = x_ref[...] + _merge(ya, yb, yc_ref[...], gain_ref[...], wo_ref[...])


def _mix_prompt(x, a, u, v, yc, wbd, ps, sw, sb, gain, wo, *, seq_len):
    t = x.shape[0]
    ts = _tile(seq_len, 512)
    hpt = ts // HALO_ROWS
    row = lambda n: pl.BlockSpec((ts, n), lambda i: (i, 0))
    full = lambda s: pl.BlockSpec(s, lambda i: (0,) * len(s))
    return pl.pallas_call(
        functools.partial(_mix_prompt_kernel, tiles_per_seq=seq_len // ts),
        grid=(t // ts,),
        in_specs=[row(D_MODEL), row(D_A),
                  pl.BlockSpec((HALO_ROWS, D_A), lambda i: (jnp.maximum(i * hpt - 1, 0), 0)),
                  row(D_B), row(D_B), row(D_C), full((D_A, D_A)), full((1, D_A)),
                  full((N_HEADS_B, SGU_CHUNK, SGU_CHUNK)), full((SGU_CHUNK, D_B)),
                  full((1, D_MODEL)), full((D_MODEL, D_MODEL))],
        out_specs=row(D_MODEL),
        out_shape=jax.ShapeDtypeStruct(x.shape, F32),
        scratch_shapes=[pltpu.VMEM((ts + HALO_ROWS, D_A), F32)],
        compiler_params=_params("arbitrary"),
        name="mix_prompt",
    )(x, a, a, u, v, yc, wbd, ps, sw, sb, gain, wo)


def _mix_sample_kernel(x_ref, a_ref, st_ref, u_ref, v_ref, yc_ref, wbd_ref, ps_ref, sw_ref, sb_ref,
                       gain_ref, wo_ref, o_ref, pool_ref, *, n_new):
    def ext(p):
        if p < POOL_STATE:
            return st_ref[:, p * D_A:(p + 1) * D_A]
        return a_ref[:, (p - POOL_STATE) * D_A:(p - POOL_STATE + 1) * D_A]

    for t in range(n_new):
        j = POOL_STATE + t
        sums, acc = [], None
        for w in range(max(POOL_WINDOWS)):
            acc = ext(j - w) if acc is None else acc + ext(j - w)
            if w + 1 in POOL_WINDOWS:
                sums.append(acc)
        ya = _pool_select(sums, _window_lanes(), ext(j), wbd_ref[...], ps_ref[...])
        s = sb_ref[t:t + 1, :]
        for k in range(t + 1):
            s = s + sw_ref[t * n_new + k:t * n_new + k + 1, :] * v_ref[:, k * D_B:(k + 1) * D_B]
        yb = u_ref[:, t * D_B:(t + 1) * D_B] * s
        yc = yc_ref[:, t * D_C:(t + 1) * D_C]
        cols = slice(t * D_MODEL, (t + 1) * D_MODEL)
        o_ref[:, cols] = x_ref[:, cols] + _merge(ya, yb, yc, gain_ref[...], wo_ref[...])
    keep = POOL_STATE - n_new
    pool_ref[:, :keep * D_A] = st_ref[:, n_new * D_A:]
    pool_ref[:, keep * D_A:] = a_ref[...]


def _mix_sample(layer, x, a, state, u, v, yc, wbd, ps, sw4, sb4, gain, wo, *, n_new):
    db = x.shape[0]
    full = lambda s: pl.BlockSpec(s, lambda i: (0,) * len(s))
    return pl.pallas_call(
        functools.partial(_mix_sample_kernel, n_new=n_new),
        grid=(1,),
        in_specs=[full(x.shape), full(a.shape),
                  pl.BlockSpec((None, db, POOL_STATE * D_A), lambda i: (layer, 0, 0)),
                  full(u.shape), full(v.shape), full(yc.shape), full((D_A, D_A)), full((1, D_A)),
                  full(sw4.shape), full(sb4.shape), full((1, D_MODEL)), full((D_MODEL, D_MODEL))],
        out_specs=[full(x.shape), full((db, POOL_STATE * D_A))],
        out_shape=[jax.ShapeDtypeStruct(x.shape, F32), jax.ShapeDtypeStruct((db, POOL_STATE * D_A), F32)],
        compiler_params=_params("arbitrary"),
        name="mix_sample",
    )(x, a, state, u, v, yc, wbd, ps, sw4, sb4, gain, wo)


def _ffn_kernel(te_ref, nt_ref, x_ref, g_ref, wg_ref, wu_ref, wd_ref, o_ref, h_ref, acc_ref, *, routed):
    del te_ref
    i = pl.program_id(0)
    j = pl.program_id(1)
    live = i < nt_ref[0]
    last = j == pl.num_programs(1) - 1

    @pl.when(jnp.logical_and(live, j == 0))
    def _():
        h_ref[...] = _rms(x_ref[...], g_ref[...]).astype(BF16)
        acc_ref[...] = jnp.zeros_like(acc_ref)

    @pl.when(live)
    def _():
        h = h_ref[...]
        gate = jnp.dot(h, wg_ref[...], preferred_element_type=F32)
        up = jnp.dot(h, wu_ref[...], preferred_element_type=F32)
        act = (_silu(gate) * up).astype(BF16)
        acc_ref[...] += jnp.dot(act, wd_ref[...], preferred_element_type=F32)

    @pl.when(jnp.logical_and(live, last))
    def _():
        o_ref[...] = acc_ref[...] if routed else x_ref[...] + acc_ref[...]

    @pl.when(jnp.logical_and(jnp.logical_not(live), last))
    def _():
        o_ref[...] = jnp.zeros_like(o_ref)


def _ffn(x, g, wg, wu, wd, li, *, tile_expert=None, n_live=None, tm):
    routed = tile_expert is not None
    r = x.shape[0]
    n_tiles = r // tm
    if not routed:
        tile_expert = jnp.zeros((n_tiles,), jnp.int32)
        n_live = jnp.full((1,), n_tiles, jnp.int32)
        wspec_in = pl.BlockSpec((None, D_MODEL, FF_CHUNK), lambda i, j, te, nt: (li, 0, j))
        wspec_out = pl.BlockSpec((None, FF_CHUNK, D_MODEL), lambda i, j, te, nt: (li, j, 0))
    else:
        wspec_in = pl.BlockSpec((None, None, D_MODEL, FF_CHUNK), lambda i, j, te, nt: (li, te[i], 0, j))
        wspec_out = pl.BlockSpec((None, None, FF_CHUNK, D_MODEL), lambda i, j, te, nt: (li, te[i], j, 0))
    row = pl.BlockSpec((tm, D_MODEL), lambda i, j, te, nt: (i, 0))
    grid_spec = pltpu.PrefetchScalarGridSpec(
        num_scalar_prefetch=2,
        grid=(n_tiles, D_FF // FF_CHUNK),
        in_specs=[row, pl.BlockSpec((1, D_MODEL), lambda i, j, te, nt: (0, 0)), wspec_in, wspec_in, wspec_out],
        out_specs=row,
        scratch_shapes=[pltpu.VMEM((tm, D_MODEL), BF16), pltpu.VMEM((tm, D_MODEL), F32)],
    )
    return pl.pallas_call(
        functools.partial(_ffn_kernel, routed=routed),
        grid_spec=grid_spec,
        out_shape=jax.ShapeDtypeStruct((r, D_MODEL), F32),
        compiler_params=_params("arbitrary", "arbitrary"),
        name="ffn_routed" if routed else "ffn_dense",
    )(tile_expert, n_live, x, g, wg, wu, wd)


def _route_kernel(x_ref, g_ref, wr_ref, idx_ref, gate_ref):
    h = _rms(x_ref[...], g_ref[...])
    logits = jnp.dot(h, wr_ref[...], precision=HIGHEST, preferred_element_type=F32)
    lane = lax.broadcasted_iota(jnp.int32, logits.shape, 1)
    lane_f = lane.astype(F32)
    logits = jnp.where(lane < N_EXPERTS, logits, -jnp.inf)
    m1 = jnp.max(logits, axis=1, keepdims=True)
    i1 = jnp.min(jnp.where(logits == m1, lane_f, float(LANES)), axis=1, keepdims=True)
    rest = jnp.where(lane_f == i1, -jnp.inf, logits)
    m2 = jnp.max(rest, axis=1, keepdims=True)
    i2 = jnp.min(jnp.where(rest == m2, lane_f, float(LANES)), axis=1, keepdims=True)
    e2 = jnp.exp(m2 - m1)
    den = 1.0 + e2
    idx_ref[...] = jnp.where(lane == 0, i1, jnp.where(lane == 1, i2, 0.0)).astype(jnp.int32)
    gate_ref[...] = jnp.where(lane == 0, 1.0 / den, jnp.where(lane == 1, e2 / den, 0.0))


def _route(x, g, wr):
    t = x.shape[0]
    tm = _tile(t, 512)
    row = lambda n: pl.BlockSpec((tm, n), lambda i: (i, 0))
    return pl.pallas_call(
        _route_kernel,
        grid=(t // tm,),
        in_specs=[row(D_MODEL), pl.BlockSpec((1, D_MODEL), lambda i: (0, 0)),
                  pl.BlockSpec((D_MODEL, LANES), lambda i: (0, 0))],
        out_specs=[row(LANES), row(LANES)],
        out_shape=[jax.ShapeDtypeStruct((t, LANES), jnp.int32), jax.ShapeDtypeStruct((t, LANES), F32)],
        compiler_params=_params("arbitrary"),
        name="route",
    )(x, g, wr)


def _dispatch_kernel(dst_ref, x_ref, init_hbm, o_hbm, sem):
    del init_hbm
    tm = x_ref.shape[0]
    base = pl.program_id(0) * tm

    def copy(r, k, slot):
        return pltpu.make_async_copy(x_ref.at[pl.ds(r, 1), :], o_hbm.at[pl.ds(slot, 1), :], sem)

    def start(r, _):
        for k in range(TOP_K):
            copy(r, k, dst_ref[(base + r) * TOP_K + k]).start(priority=k % N_DMA_PRIORITIES)
        return 0

    def wait(r, _):
        for k in range(TOP_K):
            copy(r, k, 0).wait()
        return 0

    lax.fori_loop(0, tm, start, 0, unroll=DMA_UNROLL)
    lax.fori_loop(0, tm, wait, 0, unroll=DMA_UNROLL)


def _dispatch(x, dst, rows):
    t = x.shape[0]
    tm = _tile(t, 256)
    hbm = pl.BlockSpec(memory_space=pl.ANY)
    grid_spec = pltpu.PrefetchScalarGridSpec(
        num_scalar_prefetch=1,
        grid=(t // tm,),
        in_specs=[pl.BlockSpec((tm, D_MODEL), lambda i, d: (i, 0)), hbm],
        out_specs=hbm,
        scratch_shapes=[pltpu.SemaphoreType.DMA(())],
    )
    return pl.pallas_call(
        _dispatch_kernel,
        grid_spec=grid_spec,
        out_shape=jax.ShapeDtypeStruct((rows, D_MODEL), F32),
        input_output_aliases={2: 0},
        compiler_params=_params("arbitrary"),
        name="dispatch",
    )(dst, x, jnp.zeros((rows, D_MODEL), F32))


def _combine_kernel(dst_ref, x_ref, gate_ref, y_hbm, g_ref, o_ref, buf_ref, sem, *, final):
    tm = x_ref.shape[0]
    base = pl.program_id(0) * tm

    def copy(r, k, slot):
        return pltpu.make_async_copy(y_hbm.at[pl.ds(slot, 1), :], buf_ref.at[k, pl.ds(r, 1), :], sem)

    def start(r, _):
        for k in range(TOP_K):
            copy(r, k, dst_ref[(base + r) * TOP_K + k]).start(priority=k % N_DMA_PRIORITIES)
        return 0

    def wait(r, _):
        for k in range(TOP_K):
            copy(r, k, 0).wait()
        return 0

    lax.fori_loop(0, tm, start, 0, unroll=DMA_UNROLL)
    lax.fori_loop(0, tm, wait, 0, unroll=DMA_UNROLL)
    gate = gate_ref[...]
    out = x_ref[...]
    for k in range(TOP_K):
        out = out + gate[:, k:k + 1] * buf_ref[k]
    o_ref[...] = _rms(out, g_ref[...]) if final else out


def _combine(x, gate, y, dst, g_final, *, final):
    t = x.shape[0]
    tm = _tile(t, 256)
    row = lambda n: pl.BlockSpec((tm, n), lambda i, d: (i, 0))
    grid_spec = pltpu.PrefetchScalarGridSpec(
        num_scalar_prefetch=1,
        grid=(t // tm,),
        in_specs=[row(D_MODEL), row(LANES), pl.BlockSpec(memory_space=pl.ANY),
                  pl.BlockSpec((1, D_MODEL), lambda i, d: (0, 0))],
        out_specs=row(D_MODEL),
        scratch_shapes=[pltpu.VMEM((TOP_K, tm, D_MODEL), F32), pltpu.SemaphoreType.DMA(())],
    )
    return pl.pallas_call(
        functools.partial(_combine_kernel, final=final),
        grid_spec=grid_spec,
        out_shape=jax.ShapeDtypeStruct(x.shape, F32),
        compiler_params=_params("arbitrary"),
        name="combine",
    )(dst, x, gate, y, g_final)


def _moe(x, g, wr, wg, wu, wd, li, g_final, *, tm, final):
    t = x.shape[0]
    idx, gate = _route(x, g, wr)
    e = idx[:, :TOP_K].reshape(-1)
    onehot = (e[:, None] == jnp.arange(N_EXPERTS, dtype=jnp.int32)[None, :]).astype(jnp.int32)
    rank = jnp.sum((jnp.cumsum(onehot, axis=0) - onehot) * onehot, axis=1)
    counts = jnp.sum(onehot, axis=0)
    padded = ((counts + tm - 1) // tm) * tm
    ends = jnp.cumsum(padded)
    dest = (jnp.sum(onehot * (ends - padded)[None, :], axis=1) + rank).astype(jnp.int32)
    n_tiles = (t * TOP_K) // tm + N_EXPERTS
    tile_start = jnp.arange(n_tiles, dtype=jnp.int32) * tm
    tile_expert = jnp.sum((ends[None, :] <= tile_start[:, None]).astype(jnp.int32), axis=1)
    tile_expert = jnp.minimum(tile_expert, N_EXPERTS - 1)
    n_live = (ends[-1:] // tm).astype(jnp.int32)
    xs = _dispatch(x, dest, n_tiles * tm)
    ys = _ffn(xs, g, wg, wu, wd, li, tile_expert=tile_expert, n_live=n_live, tm=tm)
    return _combine(x, gate, ys, dest, g_final, final=final)


def _final_norm_kernel(x_ref, g_ref, o_ref):
    o_ref[...] = _rms(x_ref[...], g_ref[...])


def _final_norm(x, g):
    t = x.shape[0]
    tm = _tile(t, 512)
    return pl.pallas_call(
        _final_norm_kernel,
        grid=(t // tm,),
        in_specs=[pl.BlockSpec((tm, D_MODEL), lambda i: (i, 0)), pl.BlockSpec((1, D_MODEL), lambda i: (0, 0))],
        out_specs=pl.BlockSpec((tm, D_MODEL), lambda i: (i, 0)),
        out_shape=jax.ShapeDtypeStruct(x.shape, F32),
        compiler_params=_params("arbitrary"),
        name="final_norm",
    )(x, g)


def kernel(x_prompt, x_sample, state_pool, cache_k, cache_v, cache_logf, page_table, norm_mix, w_in, b_f, pool_w, pool_scale, sgu_w, sgu_b, group_gain, w_out, norm_ffn, w_gate_dense, w_up_dense, w_down_dense, router, w_gate_moe, w_up_moe, w_down_moe, norm_final):
    depth = w_in.shape[0]
    batch, seq_len, _ = x_prompt.shape
    db, n_new, _ = x_sample.shape
    n_phys, page = cache_k.shape[1], cache_k.shape[2]
    tp = batch * seq_len

    w_in_b = jnp.pad(w_in, ((0, 0), (0, 0), (0, D_IN_PAD - w_in.shape[2]))).astype(BF16)
    bf_pad = jnp.pad(b_f, ((0, 0), (0, LANES - N_HEADS_C)))[:, None, :]
    eye = jnp.eye(len(POOL_WINDOWS), dtype=F32)
    wbd = jnp.einsum("lgcd,gh->lgchd", pool_w, eye).reshape(depth, D_A, D_A).astype(BF16)
    tri = jnp.tril(jnp.ones((SGU_CHUNK, SGU_CHUNK), bool))
    sw = jnp.where(tri, sgu_w, 0.0)
    sw_b = sw.astype(BF16)
    hd_b = D_B // N_HEADS_B
    sb_exp = jnp.repeat(jnp.swapaxes(sgu_b, 1, 2), hd_b, axis=2)
    sw4 = jnp.repeat(jnp.transpose(sw[:, :, :n_new, :n_new], (0, 2, 3, 1)), hd_b, axis=3)
    sw4 = sw4.reshape(depth, n_new * n_new, D_B)
    w_out_b = w_out.astype(BF16)
    wgd, wud, wdd = (w.astype(BF16) for w in (w_gate_dense, w_up_dense, w_down_dense))
    wgm, wum, wdm = (w.astype(BF16) for w in (w_gate_moe, w_up_moe, w_down_moe))
    router_pad = jnp.pad(router, ((0, 0), (0, 0), (0, LANES - N_EXPERTS)))
    ck_t = jnp.transpose(cache_k, (0, 1, 3, 4, 2)).reshape(depth, n_phys, D_C, page)
    cv_t = jnp.transpose(cache_v, (0, 1, 3, 4, 2)).reshape(depth, n_phys, D_C, page)
    clf_t = jnp.swapaxes(cache_logf, 2, 3)
    state = state_pool.reshape(depth, db, POOL_STATE * D_A)
    row = lambda v: v[None, :]

    xp = x_prompt.reshape(tp, D_MODEL)
    xs = x_sample.reshape(db, n_new * D_MODEL)
    pool_p, k_p, v_p, lf_p = [], [], [], []
    pool_s, sgu_s, k_s, v_s, lf_s = [], [], [], [], []
    for l in range(depth):
        gm, gf = row(norm_mix[l]), row(norm_ffn[l])
        a, u, vb, q, k, va, lf, f, ft = _in_proj(xp, gm, w_in_b[l], bf_pad[l], seq_len=seq_len)
        yc = _attn_prompt(q, k, va, f, ft, batch=batch, seq_len=seq_len)
        xp = _mix_prompt(xp, a, u, vb, yc, wbd[l], row(pool_scale[l]), sw_b[l], sb_exp[l],
                         row(group_gain[l]), w_out_b[l], seq_len=seq_len)
        pool_p.append(a.reshape(batch, seq_len, D_A)[:, -POOL_STATE:])
        k_p.append(k)
        v_p.append(va)
        lf_p.append(lf)
        a, u, vb, q, k, va, lf = _in_proj(xs.reshape(db * n_new, D_MODEL), gm, w_in_b[l], bf_pad[l])
        per_seq = lambda z: z.reshape(db, n_new, z.shape[-1])
        per_seq_t = lambda z: jnp.swapaxes(per_seq(z), 1, 2)
        yc = _attn_sample(l, per_seq(q), per_seq_t(k), per_seq_t(va), per_seq_t(lf), ck_t, cv_t, clf_t,
                          page_table)
        lanes = lambda z: z.reshape(db, -1)
        xs, pool = _mix_sample(l, xs, lanes(a), state, lanes(u), lanes(vb), lanes(yc), wbd[l],
                               row(pool_scale[l]), sw4[l], sb_exp[l, :n_new], row(group_gain[l]),
                               w_out_b[l], n_new=n_new)
        pool_s.append(pool)
        sgu_s.append(vb)
        k_s.append(k)
        v_s.append(va)
        lf_s.append(lf)
        i = l // 2
        last = l == depth - 1
        xs = xs.reshape(db * n_new, D_MODEL)
        if l % 2 == 0:
            xp = _ffn(xp, gf, wgd, wud, wdd, i, tm=_tile(tp, 512))
            xs = _ffn(xs, gf, wgd, wud, wdd, i, tm=_tile(db * n_new, 512))
            if last:
                xp, xs = _final_norm(xp, row(norm_final)), _final_norm(xs, row(norm_final))
        else:
            gfin = row(norm_final)
            xp = _moe(xp, gf, router_pad[i], wgm, wum, wdm, i, gfin, tm=_tile(tp, 512), final=last)
            xs = _moe(xs, gf, router_pad[i], wgm, wum, wdm, i, gfin, tm=_tile(db * n_new, 128), final=last)
        xs = xs.reshape(db, n_new * D_MODEL)
    st = lambda zs, shape: jnp.stack(zs, 0).reshape((depth,) + shape)
    return (xp.reshape(batch, seq_len, D_MODEL), xs.reshape(db, n_new, D_MODEL),
            st(pool_p, (batch, POOL_STATE, D_A)),
            st(k_p, (batch, seq_len, N_HEADS_C, HEAD_DIM)), st(v_p, (batch, seq_len, N_HEADS_C, HEAD_DIM)),
            st(lf_p, (batch, seq_len, N_HEADS_C)),
            st(pool_s, (db, POOL_STATE, D_A)), st(sgu_s, (db, n_new, D_B)),
            st(k_s, (db, n_new, N_HEADS_C, HEAD_DIM)), st(v_s, (db, n_new, N_HEADS_C, HEAD_DIM)),
            st(lf_s, (db, n_new, N_HEADS_C)))
```

```python
import functools

import jax
import jax.numpy as jnp
import numpy as np
from jax import lax
from jax.experimental import pallas as pl
from jax.experimental.pallas import tpu as pltpu

F32 = jnp.float32
BF16 = jnp.bfloat16
HIGHEST = lax.Precision.HIGHEST

D_MODEL = 1024
D_A = 256
D_B = 256
D_C = 512
HEAD_DIM = 64
N_HEADS_C = 8
N_HEADS_B = 4
POOL_WINDOWS = (2, 4, 8, 16)
POOL_GROUP_DIM = 64
POOL_STATE = 15
SGU_CHUNK = 128
D_FF = 2816
N_EXPERTS = 8
TOP_K = 2
EPS = 1e-6
NEG_INF = -1e30
ATTN_SCALE = HEAD_DIM ** -0.5
LOG2E = 1.4426950408889634

LANES = 128
SUBLANES = 8
HALO_ROWS = 16
OFF_U = D_A
OFF_V = OFF_U + D_B
OFF_Q = OFF_V + D_B
OFF_K = OFF_Q + D_C
OFF_VA = OFF_K + D_C
OFF_F = OFF_VA + D_C
D_IN_PAD = OFF_F + LANES
FF_CHUNK = D_FF // 2
F_SPLIT = 3
DMA_UNROLL = 8
N_DMA_PRIORITIES = 2
ATTN_TQ = 1024
ATTN_TK = 512
V7X_VMEM_BYTES = 64 * 1024 * 1024
VMEM_LIMIT = (V7X_VMEM_BYTES * 7) // 8

NT_DIMS = (((1,), (1,)), ((), ()))


def _params(*sem):
    return pltpu.CompilerParams(dimension_semantics=sem, vmem_limit_bytes=VMEM_LIMIT)


def _rms(x, g):
    return x * lax.rsqrt(jnp.mean(x * x, axis=-1, keepdims=True) + EPS) * g


def _log_sigmoid(x):
    return jnp.minimum(x, 0.0) - jnp.log1p(jnp.exp(-jnp.abs(x)))


def _silu(x):
    return x / (1.0 + jnp.exp(-x))


def _div_pow2(x, d):
    assert d & (d - 1) == 0
    return lax.shift_right_logical(x, d.bit_length() - 1)


def _tile(n, pref):
    t = min(n, pref)
    while n % t:
        t //= 2
    return t


def _in_proj_kernel(x_ref, g_ref, w_ref, bf_ref, a_ref, u_ref, v_ref, q_ref, k_ref, va_ref,
                    lf_ref, *cum, tiles_per_seq):
    h = _rms(x_ref[...], g_ref[...]).astype(BF16)
    z = jnp.dot(h, w_ref[...], preferred_element_type=F32)
    a_ref[...] = z[:, :OFF_U]
    u_ref[...] = jax.nn.gelu(z[:, OFF_U:OFF_V])
    v_ref[...] = jax.nn.gelu(z[:, OFF_V:OFF_Q])
    q_ref[...] = z[:, OFF_Q:OFF_K]
    k_ref[...] = z[:, OFF_K:OFF_VA]
    va_ref[...] = z[:, OFF_VA:OFF_F]
    lf = _log_sigmoid(z[:, OFF_F:] + bf_ref[...])
    lf_ref[...] = lf[:, :N_HEADS_C]
    if cum:
        f_ref, ft_ref, kt_ref, vt_ref, lft_ref, carry_ref = cum
        tm = lf.shape[0]
        kt_ref[...] = jnp.transpose(z[:, OFF_K:OFF_VA])
        vt_ref[...] = jnp.transpose(z[:, OFF_VA:OFF_F])
        lft_ref[...] = jnp.transpose(lf)[:N_HEADS_C, :]

        @pl.when(pl.program_id(0) % tiles_per_seq == 0)
        def _():
            carry_ref[...] = jnp.zeros_like(carry_ref)

        lane = lax.broadcasted_iota(jnp.int32, (1, LANES), 1)
        lf = jnp.where(lane < N_HEADS_C, lf, 0.0)
        r = lax.broadcasted_iota(jnp.int32, (LANES, LANES), 0)
        c = lax.broadcasted_iota(jnp.int32, (LANES, LANES), 1)
        tril = (r >= c).astype(F32)
        for sb in range(tm // LANES):
            rows = slice(sb * LANES, (sb + 1) * LANES)
            fb = jnp.dot(tril, lf[rows, :], precision=HIGHEST, preferred_element_type=F32) + carry_ref[...]
            f_ref[rows, :] = fb[:, :N_HEADS_C]
            ft_ref[:, rows] = jnp.transpose(fb)[:N_HEADS_C, :]
            carry_ref[...] = fb[LANES - 1:LANES, :]


def _in_proj(x, g, w, bf, *, seq_len=None):
    t = x.shape[0]
    tm = _tile(seq_len if seq_len else t, 512)
    row = lambda n: pl.BlockSpec((tm, n), lambda i: (i, 0))
    full = lambda s: pl.BlockSpec(s, lambda i: (0, 0))
    widths = (D_A, D_B, D_B, D_C, D_C, D_C, N_HEADS_C) + ((N_HEADS_C,) if seq_len else ())
    out_specs = [row(n) for n in widths]
    out_shape = [jax.ShapeDtypeStruct((t, n), F32) for n in widths]
    if seq_len:
        tps = seq_len // tm
        out_specs.append(pl.BlockSpec((N_HEADS_C, tm), lambda i: (0, i)))
        out_shape.append(jax.ShapeDtypeStruct((N_HEADS_C, t), F32))
        for n in (D_C, D_C, N_HEADS_C):
            out_specs.append(pl.BlockSpec((None, n, tm), lambda i: (i // tps, 0, i % tps)))
            out_shape.append(jax.ShapeDtypeStruct((t // seq_len, n, seq_len), F32))
    return pl.pallas_call(
        functools.partial(_in_proj_kernel, tiles_per_seq=(seq_len // tm) if seq_len else 1),
        grid=(t // tm,),
        in_specs=[row(D_MODEL), full((1, D_MODEL)), full((D_MODEL, D_IN_PAD)), full((1, LANES))],
        out_specs=out_specs,
        out_shape=out_shape,
        scratch_shapes=[pltpu.VMEM((1, LANES), F32)] if seq_len else [],
        compiler_params=_params("arbitrary"),
        name="in_proj",
    )(x, g, w, bf)


def _split_bf16(x):
    pieces = []
    for _ in range(F_SPLIT - 1):
        p = x.astype(BF16).astype(F32)
        pieces.append(p)
        x = x - p
    return pieces + [x]


def _attn_prompt_kernel(q_ref, k_ref, v_ref, f_ref, ft_ref, o_ref, kop_ref, vop_ref, acc_ref, *, tq, tk):
    g = pl.program_id(1)
    i = pl.program_id(2)
    per_q = tq // tk
    n_blocks = k_ref.shape[0] // tk
    lane = lax.broadcasted_iota(jnp.int32, (1, LANES), 1)
    lane8 = lax.broadcasted_iota(jnp.int32, (1, N_HEADS_C), 1)
    chan = lax.broadcasted_iota(jnp.int32, (LANES, 1), 0)
    own = (lane < HEAD_DIM, lane >= HEAD_DIM)
    spare = (HEAD_DIM, 0)

    def head_col(x8, hh):
        return jnp.sum(jnp.where(lane8 == 2 * g + hh, x8, 0.0), axis=1, keepdims=True)

    def block_rows(j):
        return pl.ds(pl.multiple_of(j * tk, tk), tk)

    @pl.when(i == 0)
    def _():
        def prepare(c, _):
            rows = block_rows(c)
            kf = k_ref[rows, :]
            vf = v_ref[rows, :]
            rel = (f_ref[rows, :] - f_ref[pl.ds(pl.multiple_of(c * tk, tk), 1), :]) * LOG2E
            for hh in range(2):
                aug = 0.0
                for n, piece in enumerate(_split_bf16(head_col(rel, hh))):
                    aug = jnp.where(lane == spare[hh] + n, -piece, aug)
                kop_ref[hh, rows, :] = jnp.where(own[hh], kf, aug).astype(BF16)
                ones = jnp.where(lane == spare[hh], 1.0, 0.0)
                vop_ref[hh, c] = jnp.transpose(jnp.where(own[hh], vf, ones)).astype(BF16)
            return 0

        lax.fori_loop(0, n_blocks, prepare, 0)

    q = q_ref[...] * (ATTN_SCALE * LOG2E)
    q_heads = []
    for hh in range(2):
        ones = jnp.where(jnp.logical_and(lane >= spare[hh], lane < spare[hh] + F_SPLIT), 1.0, 0.0)
        q_heads.append(jnp.transpose(jnp.where(own[hh], q, ones)).astype(BF16))
    qcols = pl.ds(pl.multiple_of(i * tq, tq), tq)
    krow = lax.broadcasted_iota(jnp.int32, (tk, tq), 0)
    qcol = lax.broadcasted_iota(jnp.int32, (tk, tq), 1)
    acc_ref[...] = jnp.zeros_like(acc_ref)

    def scores(j):
        return tuple(jnp.dot(kop_ref[hh, block_rows(j), :], q_heads[hh], preferred_element_type=F32)
                     for hh in range(2))

    def consume(j, s_pair, m, first_key):
        k0 = pl.multiple_of(j * tk, tk)
        m_out, probs, alphas = [], [], []
        for hh in range(2):
            h = 2 * g + hh
            s = s_pair[hh] if first_key is None else jnp.where(krow + first_key <= qcol, s_pair[hh], NEG_INF)
            start = head_col(f_ref[pl.ds(k0, 1), :], hh)
            fqj = (ft_ref[pl.ds(h, 1), qcols] - start) * LOG2E
            m_new = jnp.maximum(m[hh], jnp.max(s, axis=0, keepdims=True) + fqj)
            probs.append(jnp.exp2(s - (m_new - fqj)).astype(BF16))
            alphas.append(jnp.exp2(m[hh] - m_new))
            m_out.append(m_new)
        pv = [jnp.dot(vop_ref[hh, j], probs[hh], preferred_element_type=F32) for hh in range(2)]
        for hh in range(2):
            acc_ref[hh] = alphas[hh] * acc_ref[hh] + pv[hh]
        return tuple(m_out)

    def step(j, m):
        return consume(j, scores(j), m, None)

    first = i * per_q
    m0 = (jnp.full((1, tq), NEG_INF, F32),) * 2
    m = lax.fori_loop(0, first, step, m0)
    for d in range(per_q):
        m = consume(first + d, scores(first + d), m, d * tk)
    acc0, acc1 = acc_ref[0], acc_ref[1]
    out_t = jnp.where(chan < HEAD_DIM, acc0 / acc0[spare[0]:spare[0] + 1, :], acc1 / acc1[spare[1]:spare[1] + 1, :])
    o_ref[...] = jnp.transpose(out_t)


def _attn_prompt(q, k, v, f, ft, *, batch, seq_len):
    tq = _tile(seq_len, ATTN_TQ)
    tk = _tile(tq, ATTN_TK)
    nq = seq_len // tq
    qspec = pl.BlockSpec((tq, LANES), lambda b, g, i: (b * nq + i, g))
    kvspec = pl.BlockSpec((seq_len, LANES), lambda b, g, i: (b, g))
    return pl.pallas_call(
        functools.partial(_attn_prompt_kernel, tq=tq, tk=tk),
        grid=(batch, D_C // LANES, nq),
        in_specs=[qspec, kvspec, kvspec, pl.BlockSpec((seq_len, N_HEADS_C), lambda b, g, i: (b, 0)),
                  pl.BlockSpec((N_HEADS_C, seq_len), lambda b, g, i: (0, b))],
        out_specs=qspec,
        out_shape=jax.ShapeDtypeStruct(q.shape, F32),
        scratch_shapes=[pltpu.VMEM((2, seq_len, LANES), BF16), pltpu.VMEM((2, seq_len // tk, LANES, tk), BF16),
                        pltpu.VMEM((2, LANES, tq), F32)],
        compiler_params=_params("arbitrary", "arbitrary", "arbitrary"),
        name="attn_prompt",
    )(q, k, v, f, ft)


def _forget_consts(n_flat):
    s = np.arange(LANES)
    after = s[:, None] > s[None, :]
    r = np.arange(n_flat)
    later_page = np.logical_and(r[None, :] // N_HEADS_C > r[:, None] // N_HEADS_C,
                                r[None, :] % N_HEADS_C == r[:, None] % N_HEADS_C)
    return after.astype(np.float32), later_page.astype(np.float32)


def _page_copies(layer, seq, slot, pt_ref, caches, bufs, sems, n_pages):
    (ck, cv, clf), (kbuf, vbuf, lfbuf) = caches, bufs
    copies = []
    for p in range(n_pages):
        phys = pt_ref[seq * n_pages + p]
        copies.append(pltpu.make_async_copy(ck.at[layer, phys], kbuf.at[slot, p], sems.at[0, slot]))
        copies.append(pltpu.make_async_copy(cv.at[layer, phys], vbuf.at[slot, p], sems.at[1, slot]))
        copies.append(pltpu.make_async_copy(clf.at[layer, phys], lfbuf.at[slot, pl.ds(p * N_HEADS_C, N_HEADS_C)],
                                            sems.at[2, slot]))
    return copies


def _attn_sample_seq(slot, local, q_ref, kn_ref, vn_ref, lfn_ref, after_ref, later_ref, o_ref,
                     kbuf, vbuf, lfbuf, *, n_pages, n_new):
    n_rows = n_new * N_HEADS_C
    kbuf[slot, n_pages, :, 0:n_new] = kn_ref[local]
    vbuf[slot, n_pages, :, 0:n_new] = vn_ref[local]
    lfbuf[slot, n_pages * N_HEADS_C:(n_pages + 1) * N_HEADS_C, 0:n_new] = lfn_ref[local]

    x = lfbuf[slot]
    totals = jnp.broadcast_to(jnp.sum(x, axis=1, keepdims=True), x.shape)
    g_all = (jnp.dot(x, after_ref[...], precision=HIGHEST, preferred_element_type=F32)
             + jnp.dot(later_ref[...], totals, precision=HIGHEST, preferred_element_type=F32))

    q = q_ref[local] * ATTN_SCALE
    r8 = lax.broadcasted_iota(jnp.int32, (N_HEADS_C, D_C), 0)
    c8 = lax.broadcasted_iota(jnp.int32, (N_HEADS_C, D_C), 1)
    head_mask = _div_pow2(c8, HEAD_DIM) == r8
    qbd = jnp.concatenate(
        [jnp.where(head_mask, jnp.broadcast_to(q[t:t + 1, :], (N_HEADS_C, D_C)), 0.0) for t in range(n_new)],
        axis=0).astype(BF16)

    s_pages = []
    for p in range(n_pages + 1):
        g = g_all[p * N_HEADS_C:(p + 1) * N_HEADS_C, :]
        s = jnp.dot(qbd, kbuf[slot, p].astype(BF16), preferred_element_type=F32)
        s_pages.append(s + jnp.concatenate([g] * n_new, axis=0))
    key = lax.broadcasted_iota(jnp.int32, s_pages[0].shape, 1)
    qt = _div_pow2(lax.broadcasted_iota(jnp.int32, s_pages[0].shape, 0), N_HEADS_C)
    s_pages[n_pages] = jnp.where(key <= qt, s_pages[n_pages], NEG_INF)
    m = functools.reduce(jnp.maximum, [jnp.max(s, axis=1, keepdims=True) for s in s_pages])
    probs = [jnp.exp(s - m) for s in s_pages]
    denom = functools.reduce(jnp.add, [jnp.sum(p, axis=1, keepdims=True) for p in probs])
    out = None
    for p in range(n_pages + 1):
        d = lax.dot_general(probs[p].astype(BF16), vbuf[slot, p].astype(BF16), NT_DIMS,
                            preferred_element_type=F32)
        out = d if out is None else out + d
    out = out / denom
    for t in range(n_new):
        blk = out[t * N_HEADS_C:(t + 1) * N_HEADS_C, :]
        o_ref[local, t:t + 1, :] = jnp.sum(jnp.where(head_mask, blk, 0.0), axis=0, keepdims=True)


def _attn_sample_kernel(pt_ref, q_ref, kn_ref, vn_ref, lfn_ref, after_ref, later_ref, ck_hbm, cv_hbm, clf_hbm,
                        o_ref, kbuf, vbuf, lfbuf, sems, *, layer, n_pages, n_new):
    i = pl.program_id(0)
    n_steps = pl.num_programs(0)
    bufs = (kbuf, vbuf, lfbuf)
    copies = lambda seq, slot: _page_copies(layer, seq, slot, pt_ref, (ck_hbm, cv_hbm, clf_hbm), bufs,
                                            sems, n_pages)
    seq = functools.partial(_attn_sample_seq, q_ref=q_ref, kn_ref=kn_ref, vn_ref=vn_ref, lfn_ref=lfn_ref,
                            after_ref=after_ref, later_ref=later_ref, o_ref=o_ref, kbuf=kbuf, vbuf=vbuf,
                            lfbuf=lfbuf, n_pages=n_pages, n_new=n_new)

    @pl.when(i == 0)
    def _():
        for slot in range(2):
            kbuf[slot, n_pages] = jnp.zeros(kbuf.shape[2:], F32)
            vbuf[slot, n_pages] = jnp.zeros(vbuf.shape[2:], F32)
            lfbuf[slot, n_pages * N_HEADS_C:, :] = jnp.zeros((lfbuf.shape[1] - n_pages * N_HEADS_C, LANES), F32)
        for c in copies(0, 0):
            c.start()

    for c in copies(2 * i + 1, 1):
        c.start()
    for c in copies(2 * i, 0):
        c.wait()
    seq(0, 0)

    @pl.when(i + 1 < n_steps)
    def _():
        for c in copies(2 * i + 2, 0):
            c.start()

    for c in copies(2 * i + 1, 1):
        c.wait()
    seq(1, 1)


def _attn_sample(layer, q, k_new, v_new, lf_new, cache_k, cache_v, cache_lf, page_table):
    db, n_new, _ = q.shape
    n_pages = page_table.shape[1]
    page = cache_k.shape[3]
    assert db % 2 == 0 and page == LANES and n_new <= page
    flat_rows = -(-((n_pages + 1) * N_HEADS_C) // LANES) * LANES
    after, later = _forget_consts(flat_rows)
    per_seq = lambda shape: pl.BlockSpec((2,) + shape, lambda b, pt: (b, 0, 0))
    const = lambda n: pl.BlockSpec((n, n), lambda b, pt: (0, 0))
    hbm = pl.BlockSpec(memory_space=pl.ANY)
    grid_spec = pltpu.PrefetchScalarGridSpec(
        num_scalar_prefetch=1,
        grid=(db // 2,),
        in_specs=[per_seq((n_new, D_C)), per_seq((D_C, n_new)), per_seq((D_C, n_new)),
                  per_seq((N_HEADS_C, n_new)), const(LANES), const(flat_rows), hbm, hbm, hbm],
        out_specs=per_seq((n_new, D_C)),
        scratch_shapes=[pltpu.VMEM((2, n_pages + 1, D_C, page), F32),
                        pltpu.VMEM((2, n_pages + 1, D_C, page), F32),
                        pltpu.VMEM((2, flat_rows, LANES), F32),
                        pltpu.SemaphoreType.DMA((3, 2))],
    )
    return pl.pallas_call(
        functools.partial(_attn_sample_kernel, layer=layer, n_pages=n_pages, n_new=n_new),
        grid_spec=grid_spec,
        out_shape=jax.ShapeDtypeStruct(q.shape, F32),
        compiler_params=_params("arbitrary"),
        name="attn_sample",
    )(page_table.reshape(-1), q, k_new, v_new, lf_new, jnp.asarray(after), jnp.asarray(later),
      cache_k, cache_v, cache_lf)


def _pool_select(sums, cnt, a_new, wbd, scale):
    lane = _div_pow2(lax.broadcasted_iota(jnp.int32, (1, D_A), 1), POOL_GROUP_DIM)
    sel = jnp.where(lane == 0, sums[0], jnp.where(lane == 1, sums[1], jnp.where(lane == 2, sums[2], sums[3])))
    d = sel / cnt - a_new
    return jnp.dot(d.astype(BF16), wbd, preferred_element_type=F32) * scale


def _merge(ya, yb, yc, gain, w_out):
    y = jnp.concatenate([_rms(ya, gain[:, :D_A]), _rms(yb, gain[:, D_A:D_A + D_B]),
                         _rms(yc, gain[:, D_A + D_B:])], axis=-1)
    return jnp.dot(y.astype(BF16), w_out, preferred_element_type=F32)


def _window_lanes():
    lane = _div_pow2(lax.broadcasted_iota(jnp.int32, (1, D_A), 1), POOL_GROUP_DIM)
    return jnp.where(lane == 0, 2.0, jnp.where(lane == 1, 4.0, jnp.where(lane == 2, 8.0, 16.0)))


def _mix_prompt_kernel(x_ref, a_ref, halo_ref, u_ref, v_ref, yc_ref, wbd_ref, ps_ref, sw_ref, sb_ref,
                       gain_ref, wo_ref, o_ref, ext_ref, *, tiles_per_seq):
    ts = a_ref.shape[0]
    it = pl.program_id(0) % tiles_per_seq
    a = a_ref[...]
    ext_ref[0:HALO_ROWS, :] = jnp.where(it == 0, 0.0, halo_ref[...])
    ext_ref[HALO_ROWS:, :] = a
    sums, acc = [], None
    for j in range(max(POOL_WINDOWS)):
        sl = ext_ref[HALO_ROWS - j:HALO_ROWS - j + ts, :]
        acc = sl if acc is None else acc + sl
        if j + 1 in POOL_WINDOWS:
            sums.append(acc)
    pos1 = (it * ts + 1 + lax.broadcasted_iota(jnp.int32, (ts, 1), 0)).astype(F32)
    cnt = jnp.minimum(pos1, _window_lanes())
    ya = _pool_select(sums, cnt, a, wbd_ref[...], ps_ref[...])
    head = _div_pow2(lax.broadcasted_iota(jnp.int32, (1, D_B), 1), D_B // N_HEADS_B)
    ybs = []
    for c in range(ts // SGU_CHUNK):
        rows = slice(c * SGU_CHUNK, (c + 1) * SGU_CHUNK)
        vc = v_ref[rows, :].astype(BF16)
        zc = jnp.zeros_like(vc)
        s = sb_ref[...]
        for h in range(N_HEADS_B):
            s = s + jnp.dot(sw_ref[h], jnp.where(head == h, vc, zc), preferred_element_type=F32)
        ybs.append(u_ref[rows, :] * s)
    yb = jnp.concatenate(ybs, axis=0)
    o_ref[...] = x_ref[...] + _merge(ya, yb, yc_ref[...], gain_ref[...], wo_ref[...])


def _mix_prompt(x, a, u, v, yc, wbd, ps, sw, sb, gain, wo, *, seq_len):
    t = x.shape[0]
    ts = _tile(seq_len, 512)
    hpt = ts // HALO_ROWS
    row = lambda n: pl.BlockSpec((ts, n), lambda i: (i, 0))
    full = lambda s: pl.BlockSpec(s, lambda i: (0,) * len(s))
    return pl.pallas_call(
        functools.partial(_mix_prompt_kernel, tiles_per_seq=seq_len // ts),
        grid=(t // ts,),
        in_specs=[row(D_MODEL), row(D_A),
                  pl.BlockSpec((HALO_ROWS, D_A), lambda i: (jnp.maximum(i * hpt - 1, 0), 0)),
                  row(D_B), row(D_B), row(D_C), full((D_A, D_A)), full((1, D_A)),
                  full((N_HEADS_B, SGU_CHUNK, SGU_CHUNK)), full((SGU_CHUNK, D_B)),
                  full((1, D_MODEL)), full((D_MODEL, D_MODEL))],
        out_specs=row(D_MODEL),
        out_shape=jax.ShapeDtypeStruct(x.shape, F32),
        scratch_shapes=[pltpu.VMEM((ts + HALO_ROWS, D_A), F32)],
        compiler_params=_params("arbitrary"),
        name="mix_prompt",
    )(x, a, a, u, v, yc, wbd, ps, sw, sb, gain, wo)


def _mix_sample_kernel(x_ref, a_ref, st_ref, u_ref, v_ref, yc_ref, wbd_ref, ps_ref, sw_ref, sb_ref,
                       gain_ref, wo_ref, o_ref, pool_ref, *, n_new):
    def ext(p):
        if p < POOL_STATE:
            return st_ref[:, p * D_A:(p + 1) * D_A]
        return a_ref[:, (p - POOL_STATE) * D_A:(p - POOL_STATE + 1) * D_A]

    for t in range(n_new):
        j = POOL_STATE + t
        sums, acc = [], None
        for w in range(max(POOL_WINDOWS)):
            acc = ext(j - w) if acc is None else acc + ext(j - w)
            if w + 1 in POOL_WINDOWS:
                sums.append(acc)
        ya = _pool_select(sums, _window_lanes(), ext(j), wbd_ref[...], ps_ref[...])
        s = sb_ref[t:t + 1, :]
        for k in range(t + 1):
            s = s + sw_ref[t * n_new + k:t * n_new + k + 1, :] * v_ref[:, k * D_B:(k + 1) * D_B]
        yb = u_ref[:, t * D_B:(t + 1) * D_B] * s
        yc = yc_ref[:, t * D_C:(t + 1) * D_C]
        cols = slice(t * D_MODEL, (t + 1) * D_MODEL)
        o_ref[:, cols] = x_ref[:, cols] + _merge(ya, yb, yc, gain_ref[...], wo_ref[...])
    keep = POOL_STATE - n_new
    pool_ref[:, :keep * D_A] = st_ref[:, n_new * D_A:]
    pool_ref[:, keep * D_A:] = a_ref[...]


def _mix_sample(layer, x, a, state, u, v, yc, wbd, ps, sw4, sb4, gain, wo, *, n_new):
    db = x.shape[0]
    full = lambda s: pl.BlockSpec(s, lambda i: (0,) * len(s))
    return pl.pallas_call(
        functools.partial(_mix_sample_kernel, n_new=n_new),
        grid=(1,),
        in_specs=[full(x.shape), full(a.shape),
                  pl.BlockSpec((None, db, POOL_STATE * D_A), lambda i: (layer, 0, 0)),
                  full(u.shape), full(v.shape), full(yc.shape), full((D_A, D_A)), full((1, D_A)),
                  full(sw4.shape), full(sb4.shape), full((1, D_MODEL)), full((D_MODEL, D_MODEL))],
        out_specs=[full(x.shape), full((db, POOL_STATE * D_A))],
        out_shape=[jax.ShapeDtypeStruct(x.shape, F32), jax.ShapeDtypeStruct((db, POOL_STATE * D_A), F32)],
        compiler_params=_params("arbitrary"),
        name="mix_sample",
    )(x, a, state, u, v, yc, wbd, ps, sw4, sb4, gain, wo)


def _ffn_kernel(te_ref, nt_ref, x_ref, g_ref, wg_ref, wu_ref, wd_ref, o_ref, h_ref, acc_ref, *, routed):
    del te_ref
    i = pl.program_id(0)
    j = pl.program_id(1)
    live = i < nt_ref[0]
    last = j == pl.num_programs(1) - 1

    @pl.when(jnp.logical_and(live, j == 0))
    def _():
        h_ref[...] = _rms(x_ref[...], g_ref[...]).astype(BF16)
        acc_ref[...] = jnp.zeros_like(acc_ref)

    @pl.when(live)
    def _():
        h = h_ref[...]
        gate = jnp.dot(h, wg_ref[...], preferred_element_type=F32)
        up = jnp.dot(h, wu_ref[...], preferred_element_type=F32)
        act = (_silu(gate) * up).astype(BF16)
        acc_ref[...] += jnp.dot(act, wd_ref[...], preferred_element_type=F32)

    @pl.when(jnp.logical_and(live, last))
    def _():
        o_ref[...] = acc_ref[...] if routed else x_ref[...] + acc_ref[...]

    @pl.when(jnp.logical_and(jnp.logical_not(live), last))
    def _():
        o_ref[...] = jnp.zeros_like(o_ref)


def _ffn(x, g, wg, wu, wd, li, *, tile_expert=None, n_live=None, tm):
    routed = tile_expert is not None
    r = x.shape[0]
    n_tiles = r // tm
    if not routed:
        tile_expert = jnp.zeros((n_tiles,), jnp.int32)
        n_live = jnp.full((1,), n_tiles, jnp.int32)
        wspec_in = pl.BlockSpec((None, D_MODEL, FF_CHUNK), lambda i, j, te, nt: (li, 0, j))
        wspec_out = pl.BlockSpec((None, FF_CHUNK, D_MODEL), lambda i, j, te, nt: (li, j, 0))
    else:
        wspec_in = pl.BlockSpec((None, None, D_MODEL, FF_CHUNK), lambda i, j, te, nt: (li, te[i], 0, j))
        wspec_out = pl.BlockSpec((None, None, FF_CHUNK, D_MODEL), lambda i, j, te, nt: (li, te[i], j, 0))
    row = pl.BlockSpec((tm, D_MODEL), lambda i, j, te, nt: (i, 0))
    grid_spec = pltpu.PrefetchScalarGridSpec(
        num_scalar_prefetch=2,
        grid=(n_tiles, D_FF // FF_CHUNK),
        in_specs=[row, pl.BlockSpec((1, D_MODEL), lambda i, j, te, nt: (0, 0)), wspec_in, wspec_in, wspec_out],
        out_specs=row,
        scratch_shapes=[pltpu.VMEM((tm, D_MODEL), BF16), pltpu.VMEM((tm, D_MODEL), F32)],
    )
    return pl.pallas_call(
        functools.partial(_ffn_kernel, routed=routed),
        grid_spec=grid_spec,
        out_shape=jax.ShapeDtypeStruct((r, D_MODEL), F32),
        compiler_params=_params("arbitrary", "arbitrary"),
        name="ffn_routed" if routed else "ffn_dense",
    )(tile_expert, n_live, x, g, wg, wu, wd)


def _route_kernel(x_ref, g_ref, wr_ref, idx_ref, gate_ref):
    h = _rms(x_ref[...], g_ref[...])
    logits = jnp.dot(h, wr_ref[...], precision=HIGHEST, preferred_element_type=F32)
    lane = lax.broadcasted_iota(jnp.int32, logits.shape, 1)
    lane_f = lane.astype(F32)
    logits = jnp.where(lane < N_EXPERTS, logits, -jnp.inf)
    m1 = jnp.max(logits, axis=1, keepdims=True)
    i1 = jnp.min(jnp.where(logits == m1, lane_f, float(LANES)), axis=1, keepdims=True)
    rest = jnp.where(lane_f == i1, -jnp.inf, logits)
    m2 = jnp.max(rest, axis=1, keepdims=True)
    i2 = jnp.min(jnp.where(rest == m2, lane_f, float(LANES)), axis=1, keepdims=True)
    e2 = jnp.exp(m2 - m1)
    den = 1.0 + e2
    idx_ref[...] = jnp.where(lane == 0, i1, jnp.where(lane == 1, i2, 0.0)).astype(jnp.int32)
    gate_ref[...] = jnp.where(lane == 0, 1.0 / den, jnp.where(lane == 1, e2 / den, 0.0))


def _route(x, g, wr):
    t = x.shape[0]
    tm = _tile(t, 512)
    row = lambda n: pl.BlockSpec((tm, n), lambda i: (i, 0))
    return pl.pallas_call(
        _route_kernel,
        grid=(t // tm,),
        in_specs=[row(D_MODEL), pl.BlockSpec((1, D_MODEL), lambda i: (0, 0)),
                  pl.BlockSpec((D_MODEL, LANES), lambda i: (0, 0))],
        out_specs=[row(LANES), row(LANES)],
        out_shape=[jax.ShapeDtypeStruct((t, LANES), jnp.int32), jax.ShapeDtypeStruct((t, LANES), F32)],
        compiler_params=_params("arbitrary"),
        name="route",
    )(x, g, wr)


def _dispatch_kernel(dst_ref, x_ref, init_hbm, o_hbm, sem):
    del init_hbm
    tm = x_ref.shape[0]
    base = pl.program_id(0) * tm

    def copy(r, k, slot):
        return pltpu.make_async_copy(x_ref.at[pl.ds(r, 1), :], o_hbm.at[pl.ds(slot, 1), :], sem)

    def start(r, _):
        for k in range(TOP_K):
            copy(r, k, dst_ref[(base + r) * TOP_K + k]).start(priority=k % N_DMA_PRIORITIES)
        return 0

    def wait(r, _):
        for k in range(TOP_K):
            copy(r, k, 0).wait()
        return 0

    lax.fori_loop(0, tm, start, 0, unroll=DMA_UNROLL)
    lax.fori_loop(0, tm, wait, 0, unroll=DMA_UNROLL)


def _dispatch(x, dst, rows):
    t = x.shape[0]
    tm = _tile(t, 256)
    hbm = pl.BlockSpec(memory_space=pl.ANY)
    grid_spec = pltpu.PrefetchScalarGridSpec(
        num_scalar_prefetch=1,
        grid=(t // tm,),
        in_specs=[pl.BlockSpec((tm, D_MODEL), lambda i, d: (i, 0)), hbm],
        out_specs=hbm,
        scratch_shapes=[pltpu.SemaphoreType.DMA(())],
    )
    return pl.pallas_call(
        _dispatch_kernel,
        grid_spec=grid_spec,
        out_shape=jax.ShapeDtypeStruct((rows, D_MODEL), F32),
        input_output_aliases={2: 0},
        compiler_params=_params("arbitrary"),
        name="dispatch",
    )(dst, x, jnp.zeros((rows, D_MODEL), F32))


def _combine_kernel(dst_ref, x_ref, gate_ref, y_hbm, g_ref, o_ref, buf_ref, sem, *, final):
    tm = x_ref.shape[0]
    base = pl.program_id(0) * tm

    def copy(r, k, slot):
        return pltpu.make_async_copy(y_hbm.at[pl.ds(slot, 1), :], buf_ref.at[k, pl.ds(r, 1), :], sem)

    def start(r, _):
        for k in range(TOP_K):
            copy(r, k, dst_ref[(base + r) * TOP_K + k]).start(priority=k % N_DMA_PRIORITIES)
        return 0

    def wait(r, _):
        for k in range(TOP_K):
            copy(r, k, 0).wait()
        return 0

    lax.fori_loop(0, tm, start, 0, unroll=DMA_UNROLL)
    lax.fori_loop(0, tm, wait, 0, unroll=DMA_UNROLL)
    gate = gate_ref[...]
    out = x_ref[...]
    for k in range(TOP_K):
        out = out + gate[:, k:k + 1] * buf_ref[k]
    o_ref[...] = _rms(out, g_ref[...]) if final else out


def _combine(x, gate, y, dst, g_final, *, final):
    t = x.shape[0]
    tm = _tile(t, 256)
    row = lambda n: pl.BlockSpec((tm, n), lambda i, d: (i, 0))
    grid_spec = pltpu.PrefetchScalarGridSpec(
        num_scalar_prefetch=1,
        grid=(t // tm,),
        in_specs=[row(D_MODEL), row(LANES), pl.BlockSpec(memory_space=pl.ANY),
                  pl.BlockSpec((1, D_MODEL), lambda i, d: (0, 0))],
        out_specs=row(D_MODEL),
        scratch_shapes=[pltpu.VMEM((TOP_K, tm, D_MODEL), F32), pltpu.SemaphoreType.DMA(())],
    )
    return pl.pallas_call(
        functools.partial(_combine_kernel, final=final),
        grid_spec=grid_spec,
        out_shape=jax.ShapeDtypeStruct(x.shape, F32),
        compiler_params=_params("arbitrary"),
        name="combine",
    )(dst, x, gate, y, g_final)


def _moe(x, g, wr, wg, wu, wd, li, g_final, *, tm, final):
    t = x.shape[0]
    idx, gate = _route(x, g, wr)
    e = idx[:, :TOP_K].reshape(-1)
    onehot = (e[:, None] == jnp.arange(N_EXPERTS, dtype=jnp.int32)[None, :]).astype(jnp.int32)
    rank = jnp.sum((jnp.cumsum(onehot, axis=0) - onehot) * onehot, axis=1)
    counts = jnp.sum(onehot, axis=0)
    padded = ((counts + tm - 1) // tm) * tm
    ends = jnp.cumsum(padded)
    dest = (jnp.sum(onehot * (ends - padded)[None, :], axis=1) + rank).astype(jnp.int32)
    n_tiles = (t * TOP_K) // tm + N_EXPERTS
    tile_start = jnp.arange(n_tiles, dtype=jnp.int32) * tm
    tile_expert = jnp.sum((ends[None, :] <= tile_start[:, None]).astype(jnp.int32), axis=1)
    tile_expert = jnp.minimum(tile_expert, N_EXPERTS - 1)
    n_live = (ends[-1:] // tm).astype(jnp.int32)
    xs = _dispatch(x, dest, n_tiles * tm)
    ys = _ffn(xs, g, wg, wu, wd, li, tile_expert=tile_expert, n_live=n_live, tm=tm)
    return _combine(x, gate, ys, dest, g_final, final=final)


def _final_norm_kernel(x_ref, g_ref, o_ref):
    o_ref[...] = _rms(x_ref[...], g_ref[...])


def _final_norm(x, g):
    t = x.shape[0]
    tm = _tile(t, 512)
    return pl.pallas_call(
        _final_norm_kernel,
        grid=(t // tm,),
        in_specs=[pl.BlockSpec((tm, D_MODEL), lambda i: (i, 0)), pl.BlockSpec((1, D_MODEL), lambda i: (0, 0))],
        out_specs=pl.BlockSpec((tm, D_MODEL), lambda i: (i, 0)),
        out_shape=jax.ShapeDtypeStruct(x.shape, F32),
        compiler_params=_params("arbitrary"),
        name="final_norm",
    )(x, g)


def kernel(x_prompt, x_sample, state_pool, cache_k, cache_v, cache_logf, page_table, norm_mix, w_in, b_f, pool_w, pool_scale, sgu_w, sgu_b, group_gain, w_out, norm_ffn, w_gate_dense, w_up_dense, w_down_dense, router, w_gate_moe, w_up_moe, w_down_moe, norm_final):
    depth = w_in.shape[0]
    batch, seq_len, _ = x_prompt.shape
    db, n_new, _ = x_sample.shape
    n_phys, page = cache_k.shape[1], cache_k.shape[2]
    tp = batch * seq_len

    w_in_b = jnp.pad(w_in, ((0, 0), (0, 0), (0, D_IN_PAD - w_in.shape[2]))).astype(BF16)
    bf_pad = jnp.pad(b_f, ((0, 0), (0, LANES - N_HEADS_C)))[:, None, :]
    eye = jnp.eye(len(POOL_WINDOWS), dtype=F32)
    wbd = jnp.einsum("lgcd,gh->lgchd", pool_w, eye).reshape(depth, D_A, D_A).astype(BF16)
    tri = jnp.tril(jnp.ones((SGU_CHUNK, SGU_CHUNK), bool))
    sw = jnp.where(tri, sgu_w, 0.0)
    sw_b = sw.astype(BF16)
    hd_b = D_B // N_HEADS_B
    sb_exp = jnp.repeat(jnp.swapaxes(sgu_b, 1, 2), hd_b, axis=2)
    sw4 = jnp.repeat(jnp.transpose(sw[:, :, :n_new, :n_new], (0, 2, 3, 1)), hd_b, axis=3)
    sw4 = sw4.reshape(depth, n_new * n_new, D_B)
    w_out_b = w_out.astype(BF16)
    wgd, wud, wdd = (w.astype(BF16) for w in (w_gate_dense, w_up_dense, w_down_dense))
    wgm, wum, wdm = (w.astype(BF16) for w in (w_gate_moe, w_up_moe, w_down_moe))
    router_pad = jnp.pad(router, ((0, 0), (0, 0), (0, LANES - N_EXPERTS)))
    ck_t = jnp.transpose(cache_k, (0, 1, 3, 4, 2)).reshape(depth, n_phys, D_C, page)
    cv_t = jnp.transpose(cache_v, (0, 1, 3, 4, 2)).reshape(depth, n_phys, D_C, page)
    clf_t = jnp.swapaxes(cache_logf, 2, 3)
    state = state_pool.reshape(depth, db, POOL_STATE * D_A)
    row = lambda v: v[None, :]

    xp = x_prompt.reshape(tp, D_MODEL)
    xs = x_sample.reshape(db, n_new * D_MODEL)
    pool_p, k_p, v_p, lf_p = [], [], [], []
    pool_s, sgu_s, k_s, v_s, lf_s = [], [], [], [], []
    for l in range(depth):
        gm, gf = row(norm_mix[l]), row(norm_ffn[l])
        a, u, vb, q, k, va, lf, f, ft, kt, vt, lft = _in_proj(xp, gm, w_in_b[l], bf_pad[l], seq_len=seq_len)
        yc = _attn_prompt(q, k, va, f, ft, batch=batch, seq_len=seq_len)
        xp = _mix_prompt(xp, a, u, vb, yc, wbd[l], row(pool_scale[l]), sw_b[l], sb_exp[l],
                         row(group_gain[l]), w_out_b[l], seq_len=seq_len)
        pool_p.append(a.reshape(batch, seq_len, D_A)[:, -POOL_STATE:])
        k_p.append(kt)
        v_p.append(vt)
        lf_p.append(lft)
        a, u, vb, q, k, va, lf = _in_proj(xs.reshape(db * n_new, D_MODEL), gm, w_in_b[l], bf_pad[l])
        per_seq = lambda z: z.reshape(db, n_new, z.shape[-1])
        per_seq_t = lambda z: jnp.swapaxes(per_seq(z), 1, 2)
        yc = _attn_sample(l, per_seq(q), per_seq_t(k), per_seq_t(va), per_seq_t(lf), ck_t, cv_t, clf_t,
                          page_table)
        lanes = lambda z: z.reshape(db, -1)
        xs, pool = _mix_sample(l, xs, lanes(a), state, lanes(u), lanes(vb), lanes(yc), wbd[l],
                               row(pool_scale[l]), sw4[l], sb_exp[l, :n_new], row(group_gain[l]),
                               w_out_b[l], n_new=n_new)
        pool_s.append(pool)
        sgu_s.append(vb)
        k_s.append(k)
        v_s.append(va)
        lf_s.append(lf)
        i = l // 2
        last = l == depth - 1
        xs = xs.reshape(db * n_new, D_MODEL)
        if l % 2 == 0:
            xp = _ffn(xp, gf, wgd, wud, wdd, i, tm=_tile(tp, 512))
            xs = _ffn(xs, gf, wgd, wud, wdd, i, tm=_tile(db * n_new, 512))
            if last:
                xp, xs = _final_norm(xp, row(norm_final)), _final_norm(xs, row(norm_final))
        else:
            gfin = row(norm_final)
            xp = _moe(xp, gf, router_pad[i], wgm, wum, wdm, i, gfin, tm=_tile(tp, 512), final=last)
            xs = _moe(xs, gf, router_pad[i], wgm, wum, wdm, i, gfin, tm=_tile(db * n_new, 128), final=last)
        xs = xs.reshape(db, n_new * D_MODEL)
    st = lambda zs, shape: jnp.stack(zs, 0).reshape((depth,) + shape)

    def seq_major(zs):
        z = jnp.stack(zs, 0).reshape(depth, batch, N_HEADS_C, HEAD_DIM, seq_len)
        return jnp.transpose(z, (0, 1, 4, 2, 3))

    return (xp.reshape(batch, seq_len, D_MODEL), xs.reshape(db, n_new, D_MODEL),
            st(pool_p, (batch, POOL_STATE, D_A)),
            seq_major(k_p), seq_major(v_p), jnp.swapaxes(jnp.stack(lf_p, 0), 2, 3),
            st(pool_s, (db, POOL_STATE, D_A)), st(sgu_s, (db, n_new, D_B)),
            st(k_s, (db, n_new, N_HEADS_C, HEAD_DIM)), st(v_s, (db, n_new, N_HEADS_C, HEAD_DIM)),
            st(lf_s, (db, n_new, N_HEADS_C)))
```

```python
import functools

import jax
import jax.numpy as jnp
import numpy as np
from jax import lax
from jax.experimental import pallas as pl
from jax.experimental.pallas import tpu as pltpu

F32 = jnp.float32
BF16 = jnp.bfloat16
HIGHEST = lax.Precision.HIGHEST

D_MODEL = 1024
D_A = 256
D_B = 256
D_C = 512
HEAD_DIM = 64
N_HEADS_C = 8
N_HEADS_B = 4
POOL_WINDOWS = (2, 4, 8, 16)
POOL_GROUP_DIM = 64
POOL_STATE = 15
SGU_CHUNK = 128
D_FF = 2816
N_EXPERTS = 8
TOP_K = 2
EPS = 1e-6
NEG_INF = -1e30
ATTN_SCALE = HEAD_DIM ** -0.5
LOG2E = 1.4426950408889634

LANES = 128
SUBLANES = 8
HALO_ROWS = 16
OFF_U = D_A
OFF_V = OFF_U + D_B
OFF_Q = OFF_V + D_B
OFF_K = OFF_Q + D_C
OFF_VA = OFF_K + D_C
OFF_F = OFF_VA + D_C
D_IN_PAD = OFF_F + LANES
FF_CHUNK = D_FF // 2
ROUTED_TILE = 256
F_SPLIT = 3
DMA_UNROLL = 8
N_DMA_PRIORITIES = 2
ATTN_TQ = 1024
ATTN_TK = 512
V7X_VMEM_BYTES = 64 * 1024 * 1024
VMEM_LIMIT = (V7X_VMEM_BYTES * 7) // 8

NT_DIMS = (((1,), (1,)), ((), ()))


def _params(*sem):
    return pltpu.CompilerParams(dimension_semantics=sem, vmem_limit_bytes=VMEM_LIMIT)


def _rms(x, g):
    return x * lax.rsqrt(jnp.mean(x * x, axis=-1, keepdims=True) + EPS) * g


def _log_sigmoid(x):
    return jnp.minimum(x, 0.0) - jnp.log1p(jnp.exp(-jnp.abs(x)))


def _silu(x):
    return x / (1.0 + jnp.exp(-x))


def _div_pow2(x, d):
    assert d & (d - 1) == 0
    return lax.shift_right_logical(x, d.bit_length() - 1)


def _tile(n, pref):
    t = min(n, pref)
    while n % t:
        t //= 2
    return t


def _in_proj_kernel(x_ref, g_ref, w_ref, bf_ref, a_ref, u_ref, v_ref, q_ref, k_ref, va_ref,
                    lf_ref, *cum, tiles_per_seq):
    h = _rms(x_ref[...], g_ref[...]).astype(BF16)
    z = jnp.dot(h, w_ref[...], preferred_element_type=F32)
    a_ref[...] = z[:, :OFF_U]
    u_ref[...] = jax.nn.gelu(z[:, OFF_U:OFF_V])
    v_ref[...] = jax.nn.gelu(z[:, OFF_V:OFF_Q])
    q_ref[...] = z[:, OFF_Q:OFF_K]
    k_ref[...] = z[:, OFF_K:OFF_VA]
    va_ref[...] = z[:, OFF_VA:OFF_F]
    lf = _log_sigmoid(z[:, OFF_F:] + bf_ref[...])
    lf_ref[...] = lf[:, :N_HEADS_C]
    if cum:
        f_ref, ft_ref, kt_ref, vt_ref, lft_ref, carry_ref = cum
        tm = lf.shape[0]
        kt_ref[...] = jnp.transpose(z[:, OFF_K:OFF_VA])
        vt_ref[...] = jnp.transpose(z[:, OFF_VA:OFF_F])
        lft_ref[...] = jnp.transpose(lf)[:N_HEADS_C, :]

        @pl.when(pl.program_id(0) % tiles_per_seq == 0)
        def _():
            carry_ref[...] = jnp.zeros_like(carry_ref)

        lane = lax.broadcasted_iota(jnp.int32, (1, LANES), 1)
        lf = jnp.where(lane < N_HEADS_C, lf, 0.0)
        r = lax.broadcasted_iota(jnp.int32, (LANES, LANES), 0)
        c = lax.broadcasted_iota(jnp.int32, (LANES, LANES), 1)
        tril = (r >= c).astype(F32)
        for sb in range(tm // LANES):
            rows = slice(sb * LANES, (sb + 1) * LANES)
            fb = jnp.dot(tril, lf[rows, :], precision=HIGHEST, preferred_element_type=F32) + carry_ref[...]
            f_ref[rows, :] = fb[:, :N_HEADS_C]
            ft_ref[:, rows] = jnp.transpose(fb)[:N_HEADS_C, :]
            carry_ref[...] = fb[LANES - 1:LANES, :]


def _in_proj(x, g, w, bf, *, seq_len=None):
    t = x.shape[0]
    tm = _tile(seq_len if seq_len else t, 512)
    row = lambda n: pl.BlockSpec((tm, n), lambda i: (i, 0))
    full = lambda s: pl.BlockSpec(s, lambda i: (0, 0))
    widths = (D_A, D_B, D_B, D_C, D_C, D_C, N_HEADS_C) + ((N_HEADS_C,) if seq_len else ())
    out_specs = [row(n) for n in widths]
    out_shape = [jax.ShapeDtypeStruct((t, n), F32) for n in widths]
    if seq_len:
        tps = seq_len // tm
        out_specs.append(pl.BlockSpec((N_HEADS_C, tm), lambda i: (0, i)))
        out_shape.append(jax.ShapeDtypeStruct((N_HEADS_C, t), F32))
        for n in (D_C, D_C, N_HEADS_C):
            out_specs.append(pl.BlockSpec((None, n, tm), lambda i: (i // tps, 0, i % tps)))
            out_shape.append(jax.ShapeDtypeStruct((t // seq_len, n, seq_len), F32))
    return pl.pallas_call(
        functools.partial(_in_proj_kernel, tiles_per_seq=(seq_len // tm) if seq_len else 1),
        grid=(t // tm,),
        in_specs=[row(D_MODEL), full((1, D_MODEL)), full((D_MODEL, D_IN_PAD)), full((1, LANES))],
        out_specs=out_specs,
        out_shape=out_shape,
        scratch_shapes=[pltpu.VMEM((1, LANES), F32)] if seq_len else [],
        compiler_params=_params("arbitrary"),
        name="in_proj",
    )(x, g, w, bf)


def _split_bf16(x):
    pieces = []
    for _ in range(F_SPLIT - 1):
        p = x.astype(BF16).astype(F32)
        pieces.append(p)
        x = x - p
    return pieces + [x]


def _attn_prompt_kernel(q_ref, k_ref, v_ref, f_ref, ft_ref, o_ref, kop_ref, vop_ref, acc_ref, *, tq, tk):
    g = pl.program_id(1)
    i = pl.program_id(2)
    per_q = tq // tk
    n_blocks = k_ref.shape[0] // tk
    lane = lax.broadcasted_iota(jnp.int32, (1, LANES), 1)
    lane8 = lax.broadcasted_iota(jnp.int32, (1, N_HEADS_C), 1)
    chan = lax.broadcasted_iota(jnp.int32, (LANES, 1), 0)
    own = (lane < HEAD_DIM, lane >= HEAD_DIM)
    spare = (HEAD_DIM, 0)

    def head_col(x8, hh):
        return jnp.sum(jnp.where(lane8 == 2 * g + hh, x8, 0.0), axis=1, keepdims=True)

    def block_rows(j):
        return pl.ds(pl.multiple_of(j * tk, tk), tk)

    @pl.when(i == 0)
    def _():
        def prepare(c, _):
            rows = block_rows(c)
            kf = k_ref[rows, :]
            vf = v_ref[rows, :]
            rel = (f_ref[rows, :] - f_ref[pl.ds(pl.multiple_of(c * tk, tk), 1), :]) * LOG2E
            for hh in range(2):
                aug = 0.0
                for n, piece in enumerate(_split_bf16(head_col(rel, hh))):
                    aug = jnp.where(lane == spare[hh] + n, -piece, aug)
                kop_ref[hh, rows, :] = jnp.where(own[hh], kf, aug).astype(BF16)
                ones = jnp.where(lane == spare[hh], 1.0, 0.0)
                vop_ref[hh, c] = jnp.transpose(jnp.where(own[hh], vf, ones)).astype(BF16)
            return 0

        lax.fori_loop(0, n_blocks, prepare, 0)

    q = q_ref[...] * (ATTN_SCALE * LOG2E)
    q_heads = []
    for hh in range(2):
        ones = jnp.where(jnp.logical_and(lane >= spare[hh], lane < spare[hh] + F_SPLIT), 1.0, 0.0)
        q_heads.append(jnp.transpose(jnp.where(own[hh], q, ones)).astype(BF16))
    qcols = pl.ds(pl.multiple_of(i * tq, tq), tq)
    krow = lax.broadcasted_iota(jnp.int32, (tk, tq), 0)
    qcol = lax.broadcasted_iota(jnp.int32, (tk, tq), 1)
    acc_ref[...] = jnp.zeros_like(acc_ref)

    def scores(j):
        return tuple(jnp.dot(kop_ref[hh, block_rows(j), :], q_heads[hh], preferred_element_type=F32)
                     for hh in range(2))

    def consume(j, s_pair, m, first_key):
        k0 = pl.multiple_of(j * tk, tk)
        m_out, probs, alphas = [], [], []
        for hh in range(2):
            h = 2 * g + hh
            s = s_pair[hh] if first_key is None else jnp.where(krow + first_key <= qcol, s_pair[hh], NEG_INF)
            start = head_col(f_ref[pl.ds(k0, 1), :], hh)
            fqj = (ft_ref[pl.ds(h, 1), qcols] - start) * LOG2E
            m_new = jnp.maximum(m[hh], jnp.max(s, axis=0, keepdims=True) + fqj)
            probs.append(jnp.exp2(s - (m_new - fqj)).astype(BF16))
            alphas.append(jnp.exp2(m[hh] - m_new))
            m_out.append(m_new)
        pv = [jnp.dot(vop_ref[hh, j], probs[hh], preferred_element_type=F32) for hh in range(2)]
        for hh in range(2):
            acc_ref[hh] = alphas[hh] * acc_ref[hh] + pv[hh]
        return tuple(m_out)

    def step(j, m):
        return consume(j, scores(j), m, None)

    first = i * per_q
    m0 = (jnp.full((1, tq), NEG_INF, F32),) * 2
    m = lax.fori_loop(0, first, step, m0)
    for d in range(per_q):
        m = consume(first + d, scores(first + d), m, d * tk)
    acc0, acc1 = acc_ref[0], acc_ref[1]
    out_t = jnp.where(chan < HEAD_DIM, acc0 / acc0[spare[0]:spare[0] + 1, :], acc1 / acc1[spare[1]:spare[1] + 1, :])
    o_ref[...] = jnp.transpose(out_t)


def _attn_prompt(q, k, v, f, ft, *, batch, seq_len):
    tq = _tile(seq_len, ATTN_TQ)
    tk = _tile(tq, ATTN_TK)
    nq = seq_len // tq
    qspec = pl.BlockSpec((tq, LANES), lambda b, g, i: (b * nq + i, g))
    kvspec = pl.BlockSpec((seq_len, LANES), lambda b, g, i: (b, g))
    return pl.pallas_call(
        functools.partial(_attn_prompt_kernel, tq=tq, tk=tk),
        grid=(batch, D_C // LANES, nq),
        in_specs=[qspec, kvspec, kvspec, pl.BlockSpec((seq_len, N_HEADS_C), lambda b, g, i: (b, 0)),
                  pl.BlockSpec((N_HEADS_C, seq_len), lambda b, g, i: (0, b))],
        out_specs=qspec,
        out_shape=jax.ShapeDtypeStruct(q.shape, F32),
        scratch_shapes=[pltpu.VMEM((2, seq_len, LANES), BF16), pltpu.VMEM((2, seq_len // tk, LANES, tk), BF16),
                        pltpu.VMEM((2, LANES, tq), F32)],
        compiler_params=_params("arbitrary", "arbitrary", "arbitrary"),
        name="attn_prompt",
    )(q, k, v, f, ft)


def _forget_consts(n_flat):
    s = np.arange(LANES)
    after = s[:, None] > s[None, :]
    r = np.arange(n_flat)
    later_page = np.logical_and(r[None, :] // N_HEADS_C > r[:, None] // N_HEADS_C,
                                r[None, :] % N_HEADS_C == r[:, None] % N_HEADS_C)
    return after.astype(np.float32), later_page.astype(np.float32)


def _page_copies(layer, seq, slot, pt_ref, caches, bufs, sems, n_pages):
    (ck, cv, clf), (kbuf, vbuf, lfbuf) = caches, bufs
    copies = []
    for p in range(n_pages):
        phys = pt_ref[seq * n_pages + p]
        copies.append(pltpu.make_async_copy(ck.at[layer, phys], kbuf.at[slot, p], sems.at[0, slot]))
        copies.append(pltpu.make_async_copy(cv.at[layer, phys], vbuf.at[slot, p], sems.at[1, slot]))
        copies.append(pltpu.make_async_copy(clf.at[layer, phys], lfbuf.at[slot, pl.ds(p * N_HEADS_C, N_HEADS_C)],
                                            sems.at[2, slot]))
    return copies


def _attn_sample_seq(slot, local, q_ref, kn_ref, vn_ref, lfn_ref, after_ref, later_ref, o_ref,
                     kbuf, vbuf, lfbuf, *, n_pages, n_new):
    n_rows = n_new * N_HEADS_C
    kbuf[slot, n_pages, :, 0:n_new] = kn_ref[local]
    vbuf[slot, n_pages, :, 0:n_new] = vn_ref[local]
    lfbuf[slot, n_pages * N_HEADS_C:(n_pages + 1) * N_HEADS_C, 0:n_new] = lfn_ref[local]

    x = lfbuf[slot]
    totals = jnp.broadcast_to(jnp.sum(x, axis=1, keepdims=True), x.shape)
    g_all = (jnp.dot(x, after_ref[...], precision=HIGHEST, preferred_element_type=F32)
             + jnp.dot(later_ref[...], totals, precision=HIGHEST, preferred_element_type=F32))

    q = q_ref[local] * ATTN_SCALE
    r8 = lax.broadcasted_iota(jnp.int32, (N_HEADS_C, D_C), 0)
    c8 = lax.broadcasted_iota(jnp.int32, (N_HEADS_C, D_C), 1)
    head_mask = _div_pow2(c8, HEAD_DIM) == r8
    qbd = jnp.concatenate(
        [jnp.where(head_mask, jnp.broadcast_to(q[t:t + 1, :], (N_HEADS_C, D_C)), 0.0) for t in range(n_new)],
        axis=0).astype(BF16)

    s_pages = []
    for p in range(n_pages + 1):
        g = g_all[p * N_HEADS_C:(p + 1) * N_HEADS_C, :]
        s = jnp.dot(qbd, kbuf[slot, p].astype(BF16), preferred_element_type=F32)
        s_pages.append(s + jnp.concatenate([g] * n_new, axis=0))
    key = lax.broadcasted_iota(jnp.int32, s_pages[0].shape, 1)
    qt = _div_pow2(lax.broadcasted_iota(jnp.int32, s_pages[0].shape, 0), N_HEADS_C)
    s_pages[n_pages] = jnp.where(key <= qt, s_pages[n_pages], NEG_INF)
    m = functools.reduce(jnp.maximum, [jnp.max(s, axis=1, keepdims=True) for s in s_pages])
    probs = [jnp.exp(s - m) for s in s_pages]
    denom = functools.reduce(jnp.add, [jnp.sum(p, axis=1, keepdims=True) for p in probs])
    out = None
    for p in range(n_pages + 1):
        d = lax.dot_general(probs[p].astype(BF16), vbuf[slot, p].astype(BF16), NT_DIMS,
                            preferred_element_type=F32)
        out = d if out is None else out + d
    out = out / denom
    for t in range(n_new):
        blk = out[t * N_HEADS_C:(t + 1) * N_HEADS_C, :]
        o_ref[local, t:t + 1, :] = jnp.sum(jnp.where(head_mask, blk, 0.0), axis=0, keepdims=True)


def _attn_sample_kernel(pt_ref, q_ref, kn_ref, vn_ref, lfn_ref, after_ref, later_ref, ck_hbm, cv_hbm, clf_hbm,
                        o_ref, kbuf, vbuf, lfbuf, sems, *, layer, n_pages, n_new):
    i = pl.program_id(0)
    n_steps = pl.num_programs(0)
    bufs = (kbuf, vbuf, lfbuf)
    copies = lambda seq, slot: _page_copies(layer, seq, slot, pt_ref, (ck_hbm, cv_hbm, clf_hbm), bufs,
                                            sems, n_pages)
    seq = functools.partial(_attn_sample_seq, q_ref=q_ref, kn_ref=kn_ref, vn_ref=vn_ref, lfn_ref=lfn_ref,
                            after_ref=after_ref, later_ref=later_ref, o_ref=o_ref, kbuf=kbuf, vbuf=vbuf,
                            lfbuf=lfbuf, n_pages=n_pages, n_new=n_new)

    @pl.when(i == 0)
    def _():
        for slot in range(2):
            kbuf[slot, n_pages] = jnp.zeros(kbuf.shape[2:], F32)
            vbuf[slot, n_pages] = jnp.zeros(vbuf.shape[2:], F32)
            lfbuf[slot, n_pages * N_HEADS_C:, :] = jnp.zeros((lfbuf.shape[1] - n_pages * N_HEADS_C, LANES), F32)
        for c in copies(0, 0):
            c.start()

    for c in copies(2 * i + 1, 1):
        c.start()
    for c in copies(2 * i, 0):
        c.wait()
    seq(0, 0)

    @pl.when(i + 1 < n_steps)
    def _():
        for c in copies(2 * i + 2, 0):
            c.start()

    for c in copies(2 * i + 1, 1):
        c.wait()
    seq(1, 1)


def _attn_sample(layer, q, k_new, v_new, lf_new, cache_k, cache_v, cache_lf, page_table):
    db, n_new, _ = q.shape
    n_pages = page_table.shape[1]
    page = cache_k.shape[3]
    assert db % 2 == 0 and page == LANES and n_new <= page
    flat_rows = -(-((n_pages + 1) * N_HEADS_C) // LANES) * LANES
    after, later = _forget_consts(flat_rows)
    per_seq = lambda shape: pl.BlockSpec((2,) + shape, lambda b, pt: (b, 0, 0))
    const = lambda n: pl.BlockSpec((n, n), lambda b, pt: (0, 0))
    hbm = pl.BlockSpec(memory_space=pl.ANY)
    grid_spec = pltpu.PrefetchScalarGridSpec(
        num_scalar_prefetch=1,
        grid=(db // 2,),
        in_specs=[per_seq((n_new, D_C)), per_seq((D_C, n_new)), per_seq((D_C, n_new)),
                  per_seq((N_HEADS_C, n_new)), const(LANES), const(flat_rows), hbm, hbm, hbm],
        out_specs=per_seq((n_new, D_C)),
        scratch_shapes=[pltpu.VMEM((2, n_pages + 1, D_C, page), F32),
                        pltpu.VMEM((2, n_pages + 1, D_C, page), F32),
                        pltpu.VMEM((2, flat_rows, LANES), F32),
                        pltpu.SemaphoreType.DMA((3, 2))],
    )
    return pl.pallas_call(
        functools.partial(_attn_sample_kernel, layer=layer, n_pages=n_pages, n_new=n_new),
        grid_spec=grid_spec,
        out_shape=jax.ShapeDtypeStruct(q.shape, F32),
        compiler_params=_params("arbitrary"),
        name="attn_sample",
    )(page_table.reshape(-1), q, k_new, v_new, lf_new, jnp.asarray(after), jnp.asarray(later),
      cache_k, cache_v, cache_lf)


def _pool_select(sums, cnt, a_new, wbd, scale):
    lane = _div_pow2(lax.broadcasted_iota(jnp.int32, (1, D_A), 1), POOL_GROUP_DIM)
    sel = jnp.where(lane == 0, sums[0], jnp.where(lane == 1, sums[1], jnp.where(lane == 2, sums[2], sums[3])))
    d = sel / cnt - a_new
    return jnp.dot(d.astype(BF16), wbd, preferred_element_type=F32) * scale


def _merge(ya, yb, yc, gain, w_out):
    y = jnp.concatenate([_rms(ya, gain[:, :D_A]), _rms(yb, gain[:, D_A:D_A + D_B]),
                         _rms(yc, gain[:, D_A + D_B:])], axis=-1)
    return jnp.dot(y.astype(BF16), w_out, preferred_element_type=F32)


def _window_lanes():
    lane = _div_pow2(lax.broadcasted_iota(jnp.int32, (1, D_A), 1), POOL_GROUP_DIM)
    return jnp.where(lane == 0, 2.0, jnp.where(lane == 1, 4.0, jnp.where(lane == 2, 8.0, 16.0)))


def _mix_prompt_kernel(x_ref, a_ref, halo_ref, u_ref, v_ref, yc_ref, wbd_ref, ps_ref, sw_ref, sb_ref,
                       gain_ref, wo_ref, o_ref, ext_ref, *, tiles_per_seq):
    ts = a_ref.shape[0]
    it = pl.program_id(0) % tiles_per_seq
    a = a_ref[...]
    ext_ref[0:HALO_ROWS, :] = jnp.where(it == 0, 0.0, halo_ref[...])
    ext_ref[HALO_ROWS:, :] = a
    sums, acc = [], None
    for j in range(max(POOL_WINDOWS)):
        sl = ext_ref[HALO_ROWS - j:HALO_ROWS - j + ts, :]
        acc = sl if acc is None else acc + sl
        if j + 1 in POOL_WINDOWS:
            sums.append(acc)
    pos1 = (it * ts + 1 + lax.broadcasted_iota(jnp.int32, (ts, 1), 0)).astype(F32)
    cnt = jnp.minimum(pos1, _window_lanes())
    ya = _pool_select(sums, cnt, a, wbd_ref[...], ps_ref[...])
    head = _div_pow2(lax.broadcasted_iota(jnp.int32, (1, D_B), 1), D_B // N_HEADS_B)
    ybs = []
    for c in range(ts // SGU_CHUNK):
        rows = slice(c * SGU_CHUNK, (c + 1) * SGU_CHUNK)
        vc = v_ref[rows, :].astype(BF16)
        zc = jnp.zeros_like(vc)
        s = sb_ref[...]
        for h in range(N_HEADS_B):
            s = s + jnp.dot(sw_ref[h], jnp.where(head == h, vc, zc), preferred_element_type=F32)
        ybs.append(u_ref[rows, :] * s)
    yb = jnp.concatenate(ybs, axis=0)
    o_ref[...] = x_ref[...] + _merge(ya, yb, yc_ref[...], gain_ref[...], wo_ref[...])


def _mix_prompt(x, a, u, v, yc, wbd, ps, sw, sb, gain, wo, *, seq_len):
    t = x.shape[0]
    ts = _tile(seq_len, 512)
    hpt = ts // HALO_ROWS
    row = lambda n: pl.BlockSpec((ts, n), lambda i: (i, 0))
    full = lambda s: pl.BlockSpec(s, lambda i: (0,) * len(s))
    return pl.pallas_call(
        functools.partial(_mix_prompt_kernel, tiles_per_seq=seq_len // ts),
        grid=(t // ts,),
        in_specs=[row(D_MODEL), row(D_A),
                  pl.BlockSpec((HALO_ROWS, D_A), lambda i: (jnp.maximum(i * hpt - 1, 0), 0)),
                  row(D_B), row(D_B), row(D_C), full((D_A, D_A)), full((1, D_A)),
                  full((N_HEADS_B, SGU_CHUNK, SGU_CHUNK)), full((SGU_CHUNK, D_B)),
                  full((1, D_MODEL)), full((D_MODEL, D_MODEL))],
        out_specs=row(D_MODEL),
        out_shape=jax.ShapeDtypeStruct(x.shape, F32),
        scratch_shapes=[pltpu.VMEM((ts + HALO_ROWS, D_A), F32)],
        compiler_params=_params("arbitrary"),
        name="mix_prompt",
    )(x, a, a, u, v, yc, wbd, ps, sw, sb, gain, wo)


def _mix_sample_kernel(x_ref, a_ref, st_ref, u_ref, v_ref, yc_ref, wbd_ref, ps_ref, sw_ref, sb_ref,
                       gain_ref, wo_ref, o_ref, pool_ref, *, n_new):
    def ext(p):
        if p < POOL_STATE:
            return st_ref[:, p * D_A:(p + 1) * D_A]
        return a_ref[:, (p - POOL_STATE) * D_A:(p - POOL_STATE + 1) * D_A]

    for t in range(n_new):
        j = POOL_STATE + t
        sums, acc = [], None
        for w in range(max(POOL_WINDOWS)):
            acc = ext(j - w) if acc is None else acc + ext(j - w)
            if w + 1 in POOL_WINDOWS:
                sums.append(acc)
        ya = _pool_select(sums, _window_lanes(), ext(j), wbd_ref[...], ps_ref[...])
        s = sb_ref[t:t + 1, :]
        for k in range(t + 1):
            s = s + sw_ref[t * n_new + k:t * n_new + k + 1, :] * v_ref[:, k * D_B:(k + 1) * D_B]
        yb = u_ref[:, t * D_B:(t + 1) * D_B] * s
        yc = yc_ref[:, t * D_C:(t + 1) * D_C]
        cols = slice(t * D_MODEL, (t + 1) * D_MODEL)
        o_ref[:, cols] = x_ref[:, cols] + _merge(ya, yb, yc, gain_ref[...], wo_ref[...])
    keep = POOL_STATE - n_new
    pool_ref[:, :keep * D_A] = st_ref[:, n_new * D_A:]
    pool_ref[:, keep * D_A:] = a_ref[...]


def _mix_sample(layer, x, a, state, u, v, yc, wbd, ps, sw4, sb4, gain, wo, *, n_new):
    db = x.shape[0]
    full = lambda s: pl.BlockSpec(s, lambda i: (0,) * len(s))
    return pl.pallas_call(
        functools.partial(_mix_sample_kernel, n_new=n_new),
        grid=(1,),
        in_specs=[full(x.shape), full(a.shape),
                  pl.BlockSpec((None, db, POOL_STATE * D_A), lambda i: (layer, 0, 0)),
                  full(u.shape), full(v.shape), full(yc.shape), full((D_A, D_A)), full((1, D_A)),
                  full(sw4.shape), full(sb4.shape), full((1, D_MODEL)), full((D_MODEL, D_MODEL))],
        out_specs=[full(x.shape), full((db, POOL_STATE * D_A))],
        out_shape=[jax.ShapeDtypeStruct(x.shape, F32), jax.ShapeDtypeStruct((db, POOL_STATE * D_A), F32)],
        compiler_params=_params("arbitrary"),
        name="mix_sample",
    )(x, a, state, u, v, yc, wbd, ps, sw4, sb4, gain, wo)


def _ffn_kernel(te_ref, nt_ref, x_ref, g_ref, wg_ref, wu_ref, wd_ref, o_ref, *, routed):
    del te_ref
    live = pl.program_id(0) < nt_ref[0]

    @pl.when(live)
    def _():
        x = x_ref[...]
        h = _rms(x, g_ref[...]).astype(BF16)
        out = None if routed else x
        for c in range(D_FF // FF_CHUNK):
            cols = slice(c * FF_CHUNK, (c + 1) * FF_CHUNK)
            gate = jnp.dot(h, wg_ref[:, cols], preferred_element_type=F32)
            up = jnp.dot(h, wu_ref[:, cols], preferred_element_type=F32)
            act = (_silu(gate) * up).astype(BF16)
            d = jnp.dot(act, wd_ref[cols, :], preferred_element_type=F32)
            out = d if out is None else out + d
        o_ref[...] = out

    @pl.when(jnp.logical_not(live))
    def _():
        o_ref[...] = jnp.zeros_like(o_ref)


def _ffn(x, g, wg, wu, wd, li, *, tile_expert=None, n_live=None, tm):
    routed = tile_expert is not None
    r = x.shape[0]
    n_tiles = r // tm
    if not routed:
        tile_expert = jnp.zeros((n_tiles,), jnp.int32)
        n_live = jnp.full((1,), n_tiles, jnp.int32)
        wspec_in = pl.BlockSpec((None, D_MODEL, D_FF), lambda i, te, nt: (li, 0, 0))
        wspec_out = pl.BlockSpec((None, D_FF, D_MODEL), lambda i, te, nt: (li, 0, 0))
    else:
        wspec_in = pl.BlockSpec((None, None, D_MODEL, D_FF), lambda i, te, nt: (li, te[i], 0, 0))
        wspec_out = pl.BlockSpec((None, None, D_FF, D_MODEL), lambda i, te, nt: (li, te[i], 0, 0))
    row = pl.BlockSpec((tm, D_MODEL), lambda i, te, nt: (i, 0))
    grid_spec = pltpu.PrefetchScalarGridSpec(
        num_scalar_prefetch=2,
        grid=(n_tiles,),
        in_specs=[row, pl.BlockSpec((1, D_MODEL), lambda i, te, nt: (0, 0)), wspec_in, wspec_in, wspec_out],
        out_specs=row,
    )
    return pl.pallas_call(
        functools.partial(_ffn_kernel, routed=routed),
        grid_spec=grid_spec,
        out_shape=jax.ShapeDtypeStruct((r, D_MODEL), F32),
        compiler_params=_params("arbitrary"),
        name="ffn_routed" if routed else "ffn_dense",
    )(tile_expert, n_live, x, g, wg, wu, wd)


def _route_kernel(x_ref, g_ref, wr_ref, idx_ref, gate_ref):
    h = _rms(x_ref[...], g_ref[...])
    logits = jnp.dot(h, wr_ref[...], precision=HIGHEST, preferred_element_type=F32)
    lane = lax.broadcasted_iota(jnp.int32, logits.shape, 1)
    lane_f = lane.astype(F32)
    logits = jnp.where(lane < N_EXPERTS, logits, -jnp.inf)
    m1 = jnp.max(logits, axis=1, keepdims=True)
    i1 = jnp.min(jnp.where(logits == m1, lane_f, float(LANES)), axis=1, keepdims=True)
    rest = jnp.where(lane_f == i1, -jnp.inf, logits)
    m2 = jnp.max(rest, axis=1, keepdims=True)
    i2 = jnp.min(jnp.where(rest == m2, lane_f, float(LANES)), axis=1, keepdims=True)
    e2 = jnp.exp(m2 - m1)
    den = 1.0 + e2
    idx_ref[...] = jnp.where(lane == 0, i1, jnp.where(lane == 1, i2, 0.0)).astype(jnp.int32)
    gate_ref[...] = jnp.where(lane == 0, 1.0 / den, jnp.where(lane == 1, e2 / den, 0.0))


def _route(x, g, wr):
    t = x.shape[0]
    tm = _tile(t, 512)
    row = lambda n: pl.BlockSpec((tm, n), lambda i: (i, 0))
    return pl.pallas_call(
        _route_kernel,
        grid=(t // tm,),
        in_specs=[row(D_MODEL), pl.BlockSpec((1, D_MODEL), lambda i: (0, 0)),
                  pl.BlockSpec((D_MODEL, LANES), lambda i: (0, 0))],
        out_specs=[row(LANES), row(LANES)],
        out_shape=[jax.ShapeDtypeStruct((t, LANES), jnp.int32), jax.ShapeDtypeStruct((t, LANES), F32)],
        compiler_params=_params("arbitrary"),
        name="route",
    )(x, g, wr)


def _dispatch_kernel(dst_ref, x_ref, init_hbm, o_hbm, sem):
    del init_hbm
    tm = x_ref.shape[0]
    base = pl.program_id(0) * tm

    def copy(r, k, slot):
        return pltpu.make_async_copy(x_ref.at[pl.ds(r, 1), :], o_hbm.at[pl.ds(slot, 1), :], sem)

    def start(r, _):
        for k in range(TOP_K):
            copy(r, k, dst_ref[(base + r) * TOP_K + k]).start(priority=k % N_DMA_PRIORITIES)
        return 0

    def wait(r, _):
        for k in range(TOP_K):
            copy(r, k, 0).wait()
        return 0

    lax.fori_loop(0, tm, start, 0, unroll=DMA_UNROLL)
    lax.fori_loop(0, tm, wait, 0, unroll=DMA_UNROLL)


def _dispatch(x, dst, rows):
    t = x.shape[0]
    tm = _tile(t, 256)
    hbm = pl.BlockSpec(memory_space=pl.ANY)
    grid_spec = pltpu.PrefetchScalarGridSpec(
        num_scalar_prefetch=1,
        grid=(t // tm,),
        in_specs=[pl.BlockSpec((tm, D_MODEL), lambda i, d: (i, 0)), hbm],
        out_specs=hbm,
        scratch_shapes=[pltpu.SemaphoreType.DMA(())],
    )
    return pl.pallas_call(
        _dispatch_kernel,
        grid_spec=grid_spec,
        out_shape=jax.ShapeDtypeStruct((rows, D_MODEL), F32),
        input_output_aliases={2: 0},
        compiler_params=_params("arbitrary"),
        name="dispatch",
    )(dst, x, jnp.zeros((rows, D_MODEL), F32))


def _combine_kernel(dst_ref, x_ref, gate_ref, y_hbm, g_ref, o_ref, buf_ref, sem, *, final):
    tm = x_ref.shape[0]
    base = pl.program_id(0) * tm

    def copy(r, k, slot):
        return pltpu.make_async_copy(y_hbm.at[pl.ds(slot, 1), :], buf_ref.at[k, pl.ds(r, 1), :], sem)

    def start(r, _):
        for k in range(TOP_K):
            copy(r, k, dst_ref[(base + r) * TOP_K + k]).start(priority=k % N_DMA_PRIORITIES)
        return 0

    def wait(r, _):
        for k in range(TOP_K):
            copy(r, k, 0).wait()
        return 0

    lax.fori_loop(0, tm, start, 0, unroll=DMA_UNROLL)
    lax.fori_loop(0, tm, wait, 0, unroll=DMA_UNROLL)
    gate = gate_ref[...]
    out = x_ref[...]
    for k in range(TOP_K):
        out = out + gate[:, k:k + 1] * buf_ref[k]
    o_ref[...] = _rms(out, g_ref[...]) if final else out


def _combine(x, gate, y, dst, g_final, *, final):
    t = x.shape[0]
    tm = _tile(t, 256)
    row = lambda n: pl.BlockSpec((tm, n), lambda i, d: (i, 0))
    grid_spec = pltpu.PrefetchScalarGridSpec(
        num_scalar_prefetch=1,
        grid=(t // tm,),
        in_specs=[row(D_MODEL), row(LANES), pl.BlockSpec(memory_space=pl.ANY),
                  pl.BlockSpec((1, D_MODEL), lambda i, d: (0, 0))],
        out_specs=row(D_MODEL),
        scratch_shapes=[pltpu.VMEM((TOP_K, tm, D_MODEL), F32), pltpu.SemaphoreType.DMA(())],
    )
    return pl.pallas_call(
        functools.partial(_combine_kernel, final=final),
        grid_spec=grid_spec,
        out_shape=jax.ShapeDtypeStruct(x.shape, F32),
        compiler_params=_params("arbitrary"),
        name="combine",
    )(dst, x, gate, y, g_final)


def _moe(x, g, wr, wg, wu, wd, li, g_final, *, tm, final):
    t = x.shape[0]
    idx, gate = _route(x, g, wr)
    e = idx[:, :TOP_K].reshape(-1)
    onehot = (e[:, None] == jnp.arange(N_EXPERTS, dtype=jnp.int32)[None, :]).astype(jnp.int32)
    rank = jnp.sum((jnp.cumsum(onehot, axis=0) - onehot) * onehot, axis=1)
    counts = jnp.sum(onehot, axis=0)
    padded = ((counts + tm - 1) // tm) * tm
    ends = jnp.cumsum(padded)
    dest = (jnp.sum(onehot * (ends - padded)[None, :], axis=1) + rank).astype(jnp.int32)
    n_tiles = (t * TOP_K) // tm + N_EXPERTS
    tile_start = jnp.arange(n_tiles, dtype=jnp.int32) * tm
    tile_expert = jnp.sum((ends[None, :] <= tile_start[:, None]).astype(jnp.int32), axis=1)
    tile_expert = jnp.minimum(tile_expert, N_EXPERTS - 1)
    n_live = (ends[-1:] // tm).astype(jnp.int32)
    xs = _dispatch(x, dest, n_tiles * tm)
    ys = _ffn(xs, g, wg, wu, wd, li, tile_expert=tile_expert, n_live=n_live, tm=tm)
    return _combine(x, gate, ys, dest, g_final, final=final)


def _final_norm_kernel(x_ref, g_ref, o_ref):
    o_ref[...] = _rms(x_ref[...], g_ref[...])


def _final_norm(x, g):
    t = x.shape[0]
    tm = _tile(t, 512)
    return pl.pallas_call(
        _final_norm_kernel,
        grid=(t // tm,),
        in_specs=[pl.BlockSpec((tm, D_MODEL), lambda i: (i, 0)), pl.BlockSpec((1, D_MODEL), lambda i: (0, 0))],
        out_specs=pl.BlockSpec((tm, D_MODEL), lambda i: (i, 0)),
        out_shape=jax.ShapeDtypeStruct(x.shape, F32),
        compiler_params=_params("arbitrary"),
        name="final_norm",
    )(x, g)


def kernel(x_prompt, x_sample, state_pool, cache_k, cache_v, cache_logf, page_table, norm_mix, w_in, b_f, pool_w, pool_scale, sgu_w, sgu_b, group_gain, w_out, norm_ffn, w_gate_dense, w_up_dense, w_down_dense, router, w_gate_moe, w_up_moe, w_down_moe, norm_final):
    depth = w_in.shape[0]
    batch, seq_len, _ = x_prompt.shape
    db, n_new, _ = x_sample.shape
    n_phys, page = cache_k.shape[1], cache_k.shape[2]
    tp = batch * seq_len

    w_in_b = jnp.pad(w_in, ((0, 0), (0, 0), (0, D_IN_PAD - w_in.shape[2]))).astype(BF16)
    bf_pad = jnp.pad(b_f, ((0, 0), (0, LANES - N_HEADS_C)))[:, None, :]
    eye = jnp.eye(len(POOL_WINDOWS), dtype=F32)
    wbd = jnp.einsum("lgcd,gh->lgchd", pool_w, eye).reshape(depth, D_A, D_A).astype(BF16)
    tri = jnp.tril(jnp.ones((SGU_CHUNK, SGU_CHUNK), bool))
    sw = jnp.where(tri, sgu_w, 0.0)
    sw_b = sw.astype(BF16)
    hd_b = D_B // N_HEADS_B
    sb_exp = jnp.repeat(jnp.swapaxes(sgu_b, 1, 2), hd_b, axis=2)
    sw4 = jnp.repeat(jnp.transpose(sw[:, :, :n_new, :n_new], (0, 2, 3, 1)), hd_b, axis=3)
    sw4 = sw4.reshape(depth, n_new * n_new, D_B)
    w_out_b = w_out.astype(BF16)
    wgd, wud, wdd = (w.astype(BF16) for w in (w_gate_dense, w_up_dense, w_down_dense))
    wgm, wum, wdm = (w.astype(BF16) for w in (w_gate_moe, w_up_moe, w_down_moe))
    router_pad = jnp.pad(router, ((0, 0), (0, 0), (0, LANES - N_EXPERTS)))
    ck_t = jnp.transpose(cache_k, (0, 1, 3, 4, 2)).reshape(depth, n_phys, D_C, page)
    cv_t = jnp.transpose(cache_v, (0, 1, 3, 4, 2)).reshape(depth, n_phys, D_C, page)
    clf_t = jnp.swapaxes(cache_logf, 2, 3)
    state = state_pool.reshape(depth, db, POOL_STATE * D_A)
    row = lambda v: v[None, :]

    xp = x_prompt.reshape(tp, D_MODEL)
    xs = x_sample.reshape(db, n_new * D_MODEL)
    pool_p, k_p, v_p, lf_p = [], [], [], []
    pool_s, sgu_s, k_s, v_s, lf_s = [], [], [], [], []
    for l in range(depth):
        gm, gf = row(norm_mix[l]), row(norm_ffn[l])
        a, u, vb, q, k, va, lf, f, ft, kt, vt, lft = _in_proj(xp, gm, w_in_b[l], bf_pad[l], seq_len=seq_len)
        yc = _attn_prompt(q, k, va, f, ft, batch=batch, seq_len=seq_len)
        xp = _mix_prompt(xp, a, u, vb, yc, wbd[l], row(pool_scale[l]), sw_b[l], sb_exp[l],
                         row(group_gain[l]), w_out_b[l], seq_len=seq_len)
        pool_p.append(a.reshape(batch, seq_len, D_A)[:, -POOL_STATE:])
        k_p.append(kt)
        v_p.append(vt)
        lf_p.append(lft)
        a, u, vb, q, k, va, lf = _in_proj(xs.reshape(db * n_new, D_MODEL), gm, w_in_b[l], bf_pad[l])
        per_seq = lambda z: z.reshape(db, n_new, z.shape[-1])
        per_seq_t = lambda z: jnp.swapaxes(per_seq(z), 1, 2)
        yc = _attn_sample(l, per_seq(q), per_seq_t(k), per_seq_t(va), per_seq_t(lf), ck_t, cv_t, clf_t,
                          page_table)
        lanes = lambda z: z.reshape(db, -1)
        xs, pool = _mix_sample(l, xs, lanes(a), state, lanes(u), lanes(vb), lanes(yc), wbd[l],
                               row(pool_scale[l]), sw4[l], sb_exp[l, :n_new], row(group_gain[l]),
                               w_out_b[l], n_new=n_new)
        pool_s.append(pool)
        sgu_s.append(vb)
        k_s.append(k)
        v_s.append(va)
        lf_s.append(lf)
        i = l // 2
        last = l == depth - 1
        xs = xs.reshape(db * n_new, D_MODEL)
        if l % 2 == 0:
            xp = _ffn(xp, gf, wgd, wud, wdd, i, tm=_tile(tp, 512))
            xs = _ffn(xs, gf, wgd, wud, wdd, i, tm=_tile(db * n_new, 512))
            if last:
                xp, xs = _final_norm(xp, row(norm_final)), _final_norm(xs, row(norm_final))
        else:
            gfin = row(norm_final)
            xp = _moe(xp, gf, router_pad[i], wgm, wum, wdm, i, gfin, tm=_tile(tp, ROUTED_TILE), final=last)
            xs = _moe(xs, gf, router_pad[i], wgm, wum, wdm, i, gfin, tm=_tile(db * n_new, 128), final=last)
        xs = xs.reshape(db, n_new * D_MODEL)
    st = lambda zs, shape: jnp.stack(zs, 0).reshape((depth,) + shape)

    def seq_major(zs):
        z = jnp.stack(zs, 0).reshape(depth, batch, N_HEADS_C, HEAD_DIM, seq_len)
        return jnp.transpose(z, (0, 1, 4, 2, 3))

    return (xp.reshape(batch, seq_len, D_MODEL), xs.reshape(db, n_new, D_MODEL),
            st(pool_p, (batch, POOL_STATE, D_A)),
            seq_major(k_p), seq_major(v_p), jnp.swapaxes(jnp.stack(lf_p, 0), 2, 3),
            st(pool_s, (db, POOL_STATE, D_A)), st(sgu_s, (db, n_new, D_B)),
            st(k_s, (db, n_new, N_HEADS_C, HEAD_DIM)), st(v_s, (db, n_new, N_HEADS_C, HEAD_DIM)),
            st(lf_s, (db, n_new, N_HEADS_C)))
```
